```python
import math
import jax, jax.numpy as jnp
from jax import lax
import numpy as np

D_MODEL = 2048
BATCH = 16
SEQ = 256
DEPTH = 4
DEC_BATCH = 4
DEC_SEQ = 2048
PAST_LEN = 512

GRID_W = 64
N_EVEN = (DEPTH + 1) // 2
N_ODD = DEPTH // 2
MIX_W = D_MODEL // 2
HG_HEADS = 8
HG_DK = MIX_W // HG_HEADS
HG_DV = MIX_W // HG_HEADS
MLA_HEADS = 8
MLA_NOPE = 128
MLA_ROPE = 64
MLA_V = MIX_W // MLA_HEADS
MLA_Q_RANK = D_MODEL // 4
MLA_KV_RANK = D_MODEL // 4
MLA_SCALE = (MLA_NOPE + MLA_ROPE) ** -0.5
GDN_HEADS = 8
GDN_DK = MIX_W // GDN_HEADS
GDN_DV = MIX_W // GDN_HEADS
DT_MIN = 1e-3
DT_MAX = 1e-1
HY_CH = MIX_W
HY_EMB = 33
HY_BANDS = (HY_EMB - 1) // 2
HY_HIDDEN = 64
HY_DECAY_PCT_SHORT = 0.3
HY_DECAY_PCT_LONG = 1.5
HY_TARGET = 1e-2
HY_SHIFT = 0.05
CONV_W = 3
CHUNK = 64
Q_BLOCK = 128
ROPE_BASE = 10000.0
EPS = 1e-6

EVEN_SIZES = (MIX_W, MIX_W, MIX_W, MIX_W, MIX_W, MLA_Q_RANK, MLA_KV_RANK, MLA_ROPE, MIX_W)
ODD_SIZES = (3 * MIX_W, GDN_HEADS, GDN_HEADS, GDN_HEADS, GDN_HEADS, MIX_W, 3 * HY_CH, HY_CH)
EVEN_IN = sum(EVEN_SIZES)
ODD_IN = sum(ODD_SIZES)

kernel_name = "hybrid_diffusion_prefix_trunk_step"


def split_cols(x, sizes):
    cuts = np.cumsum(sizes)[:-1].tolist()
    return jnp.split(x, cuts, axis=-1)


def rmsnorm(x, gain):
    xf = x.astype(jnp.float32)
    y = xf * lax.rsqrt(jnp.mean(xf * xf, axis=-1, keepdims=True) + EPS)
    return (y * gain.astype(jnp.float32)).astype(x.dtype)


def l2norm(x):
    xf = x.astype(jnp.float32)
    return (xf * lax.rsqrt(jnp.sum(xf * xf, axis=-1, keepdims=True) + EPS)).astype(x.dtype)


def adaln(cond, w, b):
    mod = jax.nn.silu(cond) @ w + b
    shift, scale, gate = jnp.split(mod, 3, axis=-1)
    return shift[:, None], scale[:, None], gate[:, None]


def dwconv_centred(x, w, b=None):
    pad = CONV_W // 2
    y = lax.conv_general_dilated(x, w.astype(x.dtype)[:, None, :], window_strides=(1,),
                                 padding=[(pad, pad)], dimension_numbers=("NWC", "WIO", "NWC"),
                                 feature_group_count=x.shape[-1])
    return y if b is None else y + b.astype(x.dtype)


def to_chunks(t, n_chunks):
    bsz, _, n_heads = t.shape[:3]
    t = t.astype(jnp.float32).reshape((bsz, n_chunks, CHUNK, n_heads) + t.shape[3:])
    return jnp.moveaxis(t, (1, 3), (0, 2))


def from_chunks(t, seq_len):
    t = jnp.moveaxis(t, (0, 2), (1, 3))
    return t.reshape((t.shape[0], seq_len, t.shape[3], t.shape[4]))


def gla_chunked(q, k, v, log_f, s0):
    seq_len = v.shape[1]
    n_chunks = seq_len // CHUNK
    incl = jnp.tril(jnp.ones((CHUNK, CHUNK), dtype=bool))[:, :, None]

    def step(state, blk):
        qb, kb, vb, gb = blk
        cum = jnp.cumsum(gb, axis=2)
        rel = cum[:, :, :, None, :] - cum[:, :, None, :, :]
        decay = jnp.exp(jnp.where(incl, rel, -jnp.inf))
        scores = jnp.einsum("bhtk,bhsk,bhtsk->bhts", qb, kb, decay)
        out = (jnp.einsum("bhts,bhsv->bhtv", scores, vb)
               + jnp.einsum("bhtk,bhkv->bhtv", qb * jnp.exp(cum), state))
        last = cum[:, :, -1:, :]
        new_state = (jnp.exp(last[:, :, 0, :, None]) * state
                     + jnp.einsum("bhsk,bhsv->bhkv", kb * jnp.exp(last - cum), vb))
        return new_state, out

    blocks = tuple(to_chunks(t, n_chunks) for t in (q, k, v, log_f))
    s_fin, out = lax.scan(step, s0.astype(jnp.float32), blocks)
    return from_chunks(out, seq_len).astype(v.dtype), s_fin.astype(s0.dtype)


def gated_delta_chunked(q, k, v, g, beta, s0):
    seq_len = v.shape[1]
    n_chunks = seq_len // CHUNK
    incl = jnp.tril(jnp.ones((CHUNK, CHUNK), dtype=bool))
    strict = jnp.tril(jnp.ones((CHUNK, CHUNK), dtype=bool), -1)
    eye = jnp.eye(CHUNK, dtype=jnp.float32)

    def step(state, blk):
        qb, kb, vb, gb, bb = blk
        cum = jnp.cumsum(gb, axis=-1)
        rel = cum[..., :, None] - cum[..., None, :]
        decay = jnp.exp(jnp.where(incl, rel, -jnp.inf))
        k_beta = kb * bb[..., None]
        lower = jnp.where(strict, jnp.einsum("bhtk,bhsk->bhts", k_beta, kb) * decay, 0.0)
        t_mat = lax.linalg.triangular_solve(eye + lower, jnp.broadcast_to(eye, lower.shape),
                                            left_side=True, lower=True)
        w = jnp.einsum("bhts,bhsk->bhtk", t_mat, k_beta * jnp.exp(cum)[..., None])
        u = jnp.einsum("bhts,bhsv->bhtv", t_mat, vb * bb[..., None])
        v_new = u - jnp.einsum("bhtk,bhkv->bhtv", w, state)
        attn = jnp.einsum("bhtk,bhsk->bhts", qb, kb) * decay
        out = (jnp.einsum("bhtk,bhkv->bhtv", qb * jnp.exp(cum)[..., None], state)
               + jnp.einsum("bhts,bhsv->bhtv", attn, v_new))
        last = cum[..., -1:]
        new_state = (jnp.exp(last)[..., None] * state
                     + jnp.einsum("bhsk,bhsv->bhkv", kb * jnp.exp(last - cum)[..., None], v_new))
        return new_state, out

    blocks = tuple(to_chunks(t, n_chunks) for t in (q, k, v, g, beta))
    s_fin, out = lax.scan(step, s0.astype(jnp.float32), blocks)
    return from_chunks(out, seq_len).astype(v.dtype), s_fin.astype(s0.dtype)


def bidirectional(fn, seqs, s0):
    out_sum, finals = None, []
    for d, seq in enumerate(seqs):
        if d == 1:
            seq = tuple(jnp.flip(t, axis=1) for t in seq)
        out, s_fin = fn(*seq, s0[:, d])
        if d == 1:
            out = jnp.flip(out, axis=1)
        out_sum = out if out_sum is None else out_sum + out
        finals.append(s_fin)
    return out_sum, jnp.stack(finals, axis=1)


def axial_rope(n_tokens):
    rows = n_tokens // GRID_W
    row = jnp.repeat(jnp.arange(rows, dtype=jnp.float32), GRID_W)
    col = jnp.arange(n_tokens, dtype=jnp.int32) % GRID_W
    half = MLA_ROPE // 2
    inv_freq = ROPE_BASE ** (-jnp.arange(0, half, 2, dtype=jnp.float32) / half)
    ang = jnp.concatenate([row[:, None] * inv_freq, col.astype(jnp.float32)[:, None] * inv_freq], axis=-1)
    return jnp.cos(ang), jnp.sin(ang)


def apply_rope(x, cos, sin):
    shape = (1, x.shape[1]) + (1,) * (x.ndim - 3) + (cos.shape[-1],)
    cos, sin = cos.reshape(shape), sin.reshape(shape)
    x1, x2 = jnp.split(x.astype(jnp.float32), 2, axis=-1)
    return jnp.concatenate([x1 * cos - x2 * sin, x2 * cos + x1 * sin], axis=-1).astype(x.dtype)


def mla_attention(q_nope, q_pe, k_nope, k_pe, v):
    bsz, seq_len = q_nope.shape[:2]
    nb = seq_len // Q_BLOCK

    def blocks(t):
        return jnp.moveaxis(t.reshape((bsz, nb, Q_BLOCK) + t.shape[2:]), 1, 0)

    def one_block(args):
        qn, qp = args
        s = (jnp.einsum("bqhd,bkhd->bhqk", qn, k_nope)
             + jnp.einsum("bqhr,bkr->bhqk", qp, k_pe)).astype(jnp.float32) * MLA_SCALE
        p = jax.nn.softmax(s, axis=-1).astype(v.dtype)
        return jnp.einsum("bhqk,bkhv->bqhv", p, v)

    out = lax.map(one_block, (blocks(q_nope), blocks(q_pe)))
    return jnp.moveaxis(out, 0, 1).reshape(bsz, seq_len, -1)


def hyena_filter(n, w1, b1, w2, b2, w3):
    f32 = jnp.float32
    t = jnp.linspace(0.0, 1.0, n, dtype=f32)[:, None]
    w = 2.0 * math.pi * jnp.arange(n, dtype=f32)[:, None] / n
    f = jnp.linspace(1e-4, HY_BANDS - 1, HY_BANDS, dtype=f32)[None, :]
    z = jnp.concatenate([t, jnp.cos(f * w), -jnp.sin(f * w)], axis=-1)
    hid = jnp.sin(z @ w1.astype(f32) + b1.astype(f32))
    hid = jnp.sin(hid @ w2.astype(f32) + b2.astype(f32))
    h = hid @ w3.astype(f32)
    max_decay = math.log(HY_TARGET) / HY_DECAY_PCT_SHORT
    min_decay = math.log(HY_TARGET) / HY_DECAY_PCT_LONG
    deltas = jnp.abs(jnp.linspace(min_decay, max_decay, HY_CH, dtype=f32))
    deltas = jnp.concatenate([deltas, deltas])
    h = h * (jnp.exp(-t * deltas) + HY_SHIFT)
    h_fwd, h_bwd = jnp.split(h, 2, axis=-1)
    h_full = jnp.concatenate([h_fwd, jnp.zeros((1, HY_CH), f32), h_bwd[:0:-1]], axis=0)
    return h_full / jnp.sum(jnp.abs(h_full), axis=0, keepdims=True)


def bidir_fftconv(u, h_full, skip):
    n = u.shape[1]
    uf = u.astype(jnp.float32)
    spec = jnp.fft.rfft(uf, n=2 * n, axis=1) * jnp.fft.rfft(h_full, n=2 * n, axis=0)[None]
    y = jnp.fft.irfft(spec, n=2 * n, axis=1)[:, :n]
    return (y + uf * skip.astype(jnp.float32)).astype(u.dtype)


def even_mixer(h, w_in, w_out, lb, hg_norm, q_norm, w_uq, kv_norm, w_ukv,
               hg_s0=None, ctx_ckv=None, ctx_kpe=None):
    bsz, seq_len, _ = h.shape
    latent = ctx_ckv is not None
    q_hg, f_fw, f_bw, i_hg, g_hg, cq, ckv, kpe, g_mla = split_cols(h @ w_in, EVEN_SIZES)

    q_hg = (jax.nn.silu(q_hg) * HG_DK ** -0.5).reshape(bsz, seq_len, HG_HEADS, HG_DK)
    v_hg = i_hg.reshape(bsz, seq_len, HG_HEADS, HG_DV)
    if hg_s0 is None:
        hg_s0 = jnp.zeros((bsz, 2, HG_HEADS, HG_DK, HG_DV), h.dtype)
    seqs = []
    for d, f_raw in enumerate((f_fw, f_bw)):
        lbd = lb[d].astype(jnp.float32)
        log_f = jnp.logaddexp(jnp.log(lbd), jnp.log1p(-lbd) + jax.nn.log_sigmoid(f_raw.astype(jnp.float32)))
        log_f = log_f.reshape(bsz, seq_len, HG_HEADS, HG_DK)
        seqs.append((q_hg, 1.0 - jnp.exp(log_f), v_hg, log_f))
    o_hg, hg_fin = bidirectional(gla_chunked, seqs, hg_s0)
    o_hg = rmsnorm(o_hg, hg_norm).reshape(bsz, seq_len, MIX_W) * jax.nn.silu(g_hg)

    q = (rmsnorm(cq, q_norm) @ w_uq).reshape(bsz, seq_len, MLA_HEADS, MLA_NOPE + MLA_ROPE)
    q_nope, q_pe = q[..., :MLA_NOPE], q[..., MLA_NOPE:]
    ckv = rmsnorm(ckv, kv_norm)
    if latent:
        cos, sin = axial_rope(seq_len)
        q_pe = apply_rope(q_pe, cos, sin)
        key_ckv = jnp.concatenate([ctx_ckv, ckv], axis=1)
        key_kpe = jnp.concatenate([ctx_kpe, apply_rope(kpe, cos, sin)], axis=1)
    else:
        key_ckv, key_kpe = ckv, kpe
    kv = (key_ckv @ w_ukv).reshape(bsz, key_ckv.shape[1], MLA_HEADS, MLA_NOPE + MLA_V)
    o_mla = mla_attention(q_nope, q_pe, kv[..., :MLA_NOPE], key_kpe, kv[..., MLA_NOPE:])
    o_mla = o_mla * jax.nn.silu(g_mla)

    out = jnp.concatenate([o_hg, o_mla], axis=-1) @ w_out
    if latent:
        return out
    return out, hg_fin, ckv, kpe


def odd_mixer(h, w_in, w_out, gdn_conv, a_log, dt_bias, gdn_norm, hy_conv_w, hy_conv_b,
              hy_w1, hy_b1, hy_w2, hy_b2, hy_w3, hy_skip, gdn_s0=None):
    bsz, seq_len, _ = h.shape
    latent = gdn_s0 is not None
    qkv, a_fw, a_bw, b_fw, b_bw, z, hy_in, hy_g = split_cols(h @ w_in, ODD_SIZES)

    qkv = jax.nn.silu(dwconv_centred(qkv, gdn_conv))
    q, k, v = jnp.split(qkv, 3, axis=-1)
    q = l2norm(q.reshape(bsz, seq_len, GDN_HEADS, GDN_DK)) * GDN_DK ** -0.5
    k = l2norm(k.reshape(bsz, seq_len, GDN_HEADS, GDN_DK))
    v = v.reshape(bsz, seq_len, GDN_HEADS, GDN_DV)
    if gdn_s0 is None:
        gdn_s0 = jnp.zeros((bsz, 2, GDN_HEADS, GDN_DK, GDN_DV), h.dtype)
    seqs = []
    for d, (a_raw, b_raw) in enumerate(((a_fw, b_fw), (a_bw, b_bw))):
        g = -jnp.exp(a_log[d].astype(jnp.float32)) * jax.nn.softplus(
            a_raw.astype(jnp.float32) + dt_bias[d].astype(jnp.float32))
        beta = jax.nn.sigmoid(b_raw.astype(jnp.float32))
        seqs.append((q, k, v, g, beta))
    o_gdn, gdn_fin = bidirectional(gated_delta_chunked, seqs, gdn_s0)
    o_gdn = rmsnorm(o_gdn, gdn_norm).reshape(bsz, seq_len, MIX_W) * jax.nn.silu(z)

    v_h, x0, x1 = jnp.split(dwconv_centred(hy_in, hy_conv_w, hy_conv_b), 3, axis=-1)
    filt = hyena_filter(seq_len, hy_w1, hy_b1, hy_w2, hy_b2, hy_w3)
    o_hy = x0 * bidir_fftconv(v_h * x1, filt, hy_skip) * jax.nn.silu(hy_g)

    out = jnp.concatenate([o_gdn, o_hy], axis=-1) @ w_out
    if latent:
        return out
    return out, gdn_fin


def setup_inputs(seed: int = 0) -> dict:
    key = jax.random.key(seed)
    keys = iter(jax.random.split(key, 48))
    f32 = jnp.float32

    def nrm(shape, scale):
        return jax.random.normal(next(keys), shape, f32) * scale

    def gain(shape):
        return 1.0 + nrm(shape, 0.05)

    def unif(shape, lo, hi):
        return jax.random.uniform(next(keys), shape, f32, lo, hi)

    dt = jnp.exp(unif((N_ODD, 2, GDN_HEADS), math.log(DT_MIN), math.log(DT_MAX)))
    return {
        "x_prompt": nrm((BATCH, SEQ, D_MODEL), 1.0),
        "x_sample": nrm((DEC_BATCH, DEC_SEQ, D_MODEL), 1.0),
        "state_hgrn": nrm((DEC_BATCH, N_EVEN, 2, HG_HEADS, HG_DK, HG_DV), 0.3),
        "cache_mla_ckv": nrm((DEC_BATCH, N_EVEN, PAST_LEN, MLA_KV_RANK), 1.0),
        "cache_mla_kpe": nrm((DEC_BATCH, N_EVEN, PAST_LEN, MLA_ROPE), 1.0),
        "state_gdn": nrm((DEC_BATCH, N_ODD, 2, GDN_HEADS, GDN_DK, GDN_DV), 0.1),
        "c": nrm((DEC_BATCH, D_MODEL), 1.0),
        "c_ctx": nrm((D_MODEL,), 1.0),
        "ada_w": nrm((DEPTH, D_MODEL, 3 * D_MODEL), 0.3 * D_MODEL ** -0.5),
        "ada_b": nrm((DEPTH, 3 * D_MODEL), 0.02),
        "norm_pre": gain((DEPTH, D_MODEL)),
        "norm_post": gain((DEPTH, D_MODEL)),
        "ev_in_w": nrm((N_EVEN, D_MODEL, EVEN_IN), D_MODEL ** -0.5),
        "ev_out_w": nrm((N_EVEN, 2 * MIX_W, D_MODEL), (2 * MIX_W) ** -0.5),
        "hg_lb_logits": nrm((2, DEPTH, MIX_W), 0.5),
        "hg_norm": gain((N_EVEN, HG_DV)),
        "mla_q_norm": gain((N_EVEN, MLA_Q_RANK)),
        "mla_w_uq": nrm((N_EVEN, MLA_Q_RANK, MLA_HEADS * (MLA_NOPE + MLA_ROPE)), MLA_Q_RANK ** -0.5),
        "mla_kv_norm": gain((N_EVEN, MLA_KV_RANK)),
        "mla_w_ukv": nrm((N_EVEN, MLA_KV_RANK, MLA_HEADS * (MLA_NOPE + MLA_V)), MLA_KV_RANK ** -0.5),
        "od_in_w": nrm((N_ODD, D_MODEL, ODD_IN), D_MODEL ** -0.5),
        "od_out_w": nrm((N_ODD, 2 * MIX_W, D_MODEL), (2 * MIX_W) ** -0.5),
        "gdn_conv": nrm((N_ODD, CONV_W, 3 * MIX_W), CONV_W ** -0.5),
        "gdn_a_log": jnp.log(unif((N_ODD, 2, GDN_HEADS), 1.0, 16.0)),
        "gdn_dt_bias": dt + jnp.log(-jnp.expm1(-dt)),
        "gdn_norm": gain((N_ODD, GDN_DV)),
        "hy_conv_w": nrm((N_ODD, CONV_W, 3 * HY_CH), CONV_W ** -0.5),
        "hy_conv_b": nrm((N_ODD, 3 * HY_CH), 0.02),
        "hy_w1": nrm((N_ODD, HY_EMB, HY_HIDDEN), HY_EMB ** -0.5),
        "hy_b1": nrm((N_ODD, HY_HIDDEN), 0.02),
        "hy_w2": nrm((N_ODD, HY_HIDDEN, HY_HIDDEN), HY_HIDDEN ** -0.5),
        "hy_b2": nrm((N_ODD, HY_HIDDEN), 0.02),
        "hy_w3": nrm((N_ODD, HY_HIDDEN, 2 * HY_CH), HY_HIDDEN ** -0.5),
        "hy_skip": nrm((N_ODD, HY_CH), 0.5),
    }


def reference(x_prompt, x_sample, state_hgrn, cache_mla_ckv, cache_mla_kpe, state_gdn, c, c_ctx,
              ada_w, ada_b, norm_pre, norm_post, ev_in_w, ev_out_w, hg_lb_logits, hg_norm,
              mla_q_norm, mla_w_uq, mla_kv_norm, mla_w_ukv, od_in_w, od_out_w, gdn_conv, gdn_a_log,
              gdn_dt_bias, gdn_norm, hy_conv_w, hy_conv_b, hy_w1, hy_b1, hy_w2, hy_b2, hy_w3, hy_skip):
    lb_cum = jnp.cumsum(jax.nn.softmax(hg_lb_logits.astype(jnp.float32), axis=1), axis=1)
    hg_lb = lb_cum - lb_cum[:, :1]

    xp, xs = x_prompt, x_sample
    new_hg, new_ckv, new_kpe, new_gdn = [], [], [], []
    for l in range(DEPTH):
        shift_p, scale_p, gate_p = adaln(c_ctx[None], ada_w[l], ada_b[l])
        shift_s, scale_s, gate_s = adaln(c, ada_w[l], ada_b[l])
        hp = rmsnorm(xp, norm_pre[l]) * (1.0 + scale_p) + shift_p
        hs = rmsnorm(xs, norm_pre[l]) * (1.0 + scale_s) + shift_s
        if l % 2 == 0:
            e = l // 2
            ew = (ev_in_w[e], ev_out_w[e], hg_lb[:, l], hg_norm[e], mla_q_norm[e], mla_w_uq[e],
                  mla_kv_norm[e], mla_w_ukv[e])
            out_p, s_hg, ckv, kpe = even_mixer(hp, *ew)
            out_s = even_mixer(hs, *ew, state_hgrn[:, e], cache_mla_ckv[:, e], cache_mla_kpe[:, e])
            new_hg.append(s_hg)
            new_ckv.append(ckv)
            new_kpe.append(kpe)
        else:
            o = l // 2
            ow = (od_in_w[o], od_out_w[o], gdn_conv[o], gdn_a_log[o], gdn_dt_bias[o], gdn_norm[o],
                  hy_conv_w[o], hy_conv_b[o], hy_w1[o], hy_b1[o], hy_w2[o], hy_b2[o], hy_w3[o], hy_skip[o])
            out_p, s_gdn = odd_mixer(hp, *ow)
            out_s = odd_mixer(hs, *ow, state_gdn[:, o])
            new_gdn.append(s_gdn)
        xp = xp + gate_p * rmsnorm(out_p, norm_post[l])
        xs = xs + gate_s * rmsnorm(out_s, norm_post[l])

    y_prompt, y_sample = xp, xs
    return (y_prompt, y_sample, jnp.stack(new_hg, axis=1), jnp.stack(new_ckv, axis=1),
            jnp.stack(new_kpe, axis=1), jnp.stack(new_gdn, axis=1))
```

```python
import functools
import math

import jax
import jax.numpy as jnp
import numpy as np
from jax import lax
from jax.experimental import pallas as pl
from jax.experimental.pallas import tpu as pltpu

D_MODEL = 2048
DEPTH = 4
GRID_W = 64
N_EVEN = (DEPTH + 1) // 2
N_ODD = DEPTH // 2
MIX_W = D_MODEL // 2
HG_HEADS = 8
HG_DK = MIX_W // HG_HEADS
HG_DV = MIX_W // HG_HEADS
MLA_HEADS = 8
MLA_NOPE = 128
MLA_ROPE = 64
MLA_V = MIX_W // MLA_HEADS
MLA_Q_RANK = D_MODEL // 4
MLA_KV_RANK = D_MODEL // 4
MLA_SCALE = (MLA_NOPE + MLA_ROPE) ** -0.5
GDN_HEADS = 8
GDN_DK = MIX_W // GDN_HEADS
GDN_DV = MIX_W // GDN_HEADS
HY_CH = MIX_W
HY_EMB = 33
HY_BANDS = (HY_EMB - 1) // 2
HY_DECAY_PCT_SHORT = 0.3
HY_DECAY_PCT_LONG = 1.5
HY_TARGET = 1e-2
HY_SHIFT = 0.05
CONV_W = 3
CHUNK = 64
ROPE_BASE = 10000.0
EPS = 1e-6

EVEN_IN = 5 * MIX_W + MLA_Q_RANK + MLA_KV_RANK + MLA_ROPE + MIX_W
ODD_IN = 3 * MIX_W + 4 * GDN_HEADS + MIX_W + 3 * HY_CH + HY_CH

LANES = 128
VMEM_LIMIT_BYTES = 56 * 1024 * 1024

TOKEN_TILE = 256
EVEN_IN_PAD = 3 * 19 * LANES
EVEN_TN = 19 * LANES
ODD_IN_PAD = 5 * 13 * LANES
ODD_TN = 13 * LANES
MLA_TQ = 256
KV_ROWS = 512

F32 = jnp.float32
BF16 = jnp.bfloat16
HIGHEST = lax.Precision.HIGHEST


def _cparams(*sem):
    return pltpu.CompilerParams(dimension_semantics=sem, vmem_limit_bytes=VMEM_LIMIT_BYTES)


def _dot_nt(a, b):
    return lax.dot_general(a, b, (((1,), (1,)), ((), ())), preferred_element_type=F32)


def _dot_tn(a, b):
    return lax.dot_general(a, b, (((0,), (0,)), ((), ())), preferred_element_type=F32)


def _silu(x):
    return x * jax.nn.sigmoid(x)


def _split_bf16(a):
    hi = a.astype(BF16)
    return hi, (a - hi.astype(F32)).astype(BF16)


def _dot3(a, b):
    a_hi, a_lo = _split_bf16(a)
    b_hi, b_lo = _split_bf16(b)
    return (jnp.dot(a_hi, b_hi, preferred_element_type=F32) + jnp.dot(a_hi, b_lo, preferred_element_type=F32)
            + jnp.dot(a_lo, b_hi, preferred_element_type=F32))


def _adaln_kernel(cond_ref, w_ref, b_ref, o_ref):
    act = _silu(cond_ref[...]).astype(BF16)
    o_ref[0] = jnp.dot(act, w_ref[0].astype(BF16), preferred_element_type=F32) + b_ref[0]


def adaln_all(cond, ada_w, ada_b):
    rows = cond.shape[0]
    tn = 1024
    n3 = ada_w.shape[-1]
    return pl.pallas_call(
        _adaln_kernel,
        grid=(ada_w.shape[0], n3 // tn),
        in_specs=[
            pl.BlockSpec((rows, D_MODEL), lambda l, j: (0, 0)),
            pl.BlockSpec((1, D_MODEL, tn), lambda l, j: (l, 0, j)),
            pl.BlockSpec((1, 1, tn), lambda l, j: (l, 0, j)),
        ],
        out_specs=pl.BlockSpec((1, rows, tn), lambda l, j: (l, 0, j)),
        out_shape=jax.ShapeDtypeStruct((ada_w.shape[0], rows, n3), F32),
        compiler_params=_cparams("arbitrary", "arbitrary"),
        name="adaln_mod",
    )(cond, ada_w, ada_b.reshape(ada_b.shape[0], 1, n3))


def _in_proj_kernel(x_ref, gain_ref, scale_ref, shift_ref, w_ref, o_ref):
    x = x_ref[...]
    y = x * lax.rsqrt(jnp.mean(x * x, axis=-1, keepdims=True) + EPS)
    y = y * gain_ref[...]
    h = y * (1.0 + scale_ref[0]) + shift_ref[0]
    o_ref[...] = jnp.dot(h.astype(BF16), w_ref[...], preferred_element_type=F32)


def in_proj(x2d, gain, scale, shift, w_bf16, tn, tokens_per_mod_row):
    t, d = x2d.shape
    n = w_bf16.shape[1]
    tiles_per_row = tokens_per_mod_row // TOKEN_TILE
    return pl.pallas_call(
        _in_proj_kernel,
        grid=(n // tn, t // TOKEN_TILE),
        in_specs=[
            pl.BlockSpec((TOKEN_TILE, d), lambda j, i: (i, 0)),
            pl.BlockSpec((1, d), lambda j, i: (0, 0)),
            pl.BlockSpec((1, 1, d), lambda j, i: (i // tiles_per_row, 0, 0)),
            pl.BlockSpec((1, 1, d), lambda j, i: (i // tiles_per_row, 0, 0)),
            pl.BlockSpec((d, tn), lambda j, i: (0, j)),
        ],
        out_specs=pl.BlockSpec((TOKEN_TILE, tn), lambda j, i: (i, j)),
        out_shape=jax.ShapeDtypeStruct((t, n), F32),
        compiler_params=_cparams("arbitrary", "arbitrary"),
        name="in_proj",
    )(x2d, gain, scale, shift, w_bf16)


def _out_proj_kernel(oa_ref, ob_ref, wa_ref, wb_ref, gain_ref, gate_ref, x_ref, y_ref):
    acc = (jnp.dot(oa_ref[...], wa_ref[...], preferred_element_type=F32)
           + jnp.dot(ob_ref[...], wb_ref[...], preferred_element_type=F32))
    nrm = acc * lax.rsqrt(jnp.mean(acc * acc, axis=-1, keepdims=True) + EPS) * gain_ref[...]
    y_ref[...] = x_ref[...] + gate_ref[0] * nrm


def out_proj(oa, ob, w_bf16, gain, gate, x2d, tokens_per_mod_row):
    t, d = x2d.shape
    tiles_per_row = tokens_per_mod_row // TOKEN_TILE
    return pl.pallas_call(
        _out_proj_kernel,
        grid=(t // TOKEN_TILE,),
        in_specs=[
            pl.BlockSpec((TOKEN_TILE, MIX_W), lambda i: (i, 0)),
            pl.BlockSpec((TOKEN_TILE, MIX_W), lambda i: (i, 0)),
            pl.BlockSpec((MIX_W, d), lambda i: (0, 0)),
            pl.BlockSpec((MIX_W, d), lambda i: (1, 0)),
            pl.BlockSpec((1, d), lambda i: (0, 0)),
            pl.BlockSpec((1, 1, d), lambda i: (i // tiles_per_row, 0, 0)),
            pl.BlockSpec((TOKEN_TILE, d), lambda i: (i, 0)),
        ],
        out_specs=pl.BlockSpec((TOKEN_TILE, d), lambda i: (i, 0)),
        out_shape=jax.ShapeDtypeStruct((t, d), F32),
        compiler_params=_cparams("arbitrary"),
        name="out_proj",
    )(oa, ob, w_bf16, w_bf16, gain, gate, x2d)


def _gla_kernel(*refs, n_chunks, has_s0, want_state):
    q_ref, ffw_ref, fbw_ref, v_ref, g_ref, lb_ref, nrm_ref = refs[:7]
    pos = 7
    s0_ref = None
    if has_s0:
        s0_ref = refs[pos]
        pos += 1
    o_ref = refs[pos]
    pos += 1
    sfin_ref = None
    if want_state:
        sfin_ref = refs[pos]
        pos += 1
    ofwd_scr, st_scr = refs[pos], refs[pos + 1]

    r64 = lax.broadcasted_iota(jnp.int32, (CHUNK, CHUNK), 0)
    c64 = lax.broadcasted_iota(jnp.int32, (CHUNK, CHUNK), 1)
    row128 = lax.broadcasted_iota(jnp.int32, (CHUNK, LANES), 0)
    nrm = nrm_ref[...]

    for d in (0, 1):
        rev = d == 1
        f_ref = fbw_ref if rev else ffw_ref
        lbd = lb_ref[d:d + 1, :]
        log_lb = jnp.log(lbd)
        log_1m = jnp.log1p(-lbd)
        tau_r = (CHUNK - 1 - r64) if rev else r64
        tau_c = (CHUNK - 1 - c64) if rev else c64
        tau_row = (CHUNK - 1 - row128) if rev else row128
        tri = (tau_c <= tau_r).astype(F32)
        eye = r64 == c64
        if has_s0:
            st_scr[...] = s0_ref[0, d, 0].T
        else:
            st_scr[...] = jnp.zeros_like(st_scr)

        def chunk_body(ci, carry, rev=rev, f_ref=f_ref, log_lb=log_lb, log_1m=log_1m,
                       tau_r=tau_r, tau_c=tau_c, tau_row=tau_row, tri=tri, eye=eye):
            cidx = (n_chunks - 1 - ci) if rev else ci
            rows = pl.ds(pl.multiple_of(cidx * CHUNK, CHUNK), CHUNK)
            qs = _silu(q_ref[0, rows, :]) * HG_DK ** -0.5
            vc = v_ref[0, rows, :]
            f_raw = f_ref[0, rows, :]
            log_sig = jnp.minimum(f_raw, 0.0) - jnp.log1p(jnp.exp(-jnp.abs(f_raw)))
            t = log_1m + log_sig
            log_f = jnp.maximum(log_lb, t) + jnp.log1p(jnp.exp(-jnp.abs(log_lb - t)))
            kc = 1.0 - jnp.exp(log_f)
            cum = jnp.dot(tri, log_f, precision=HIGHEST, preferred_element_type=F32)
            total = cum[0:1, :] if rev else cum[CHUNK - 1:CHUNK, :]

            a_mat = jnp.where(eye, jnp.sum(qs * kc, axis=-1, keepdims=True), 0.0)
            b_m = cum
            for lvl in range(6):
                m = 1 << lvl
                prev_b = pltpu.roll(b_m, (CHUNK - m) if rev else m, axis=0)
                qt = qs * jnp.exp(jnp.minimum(cum - prev_b, 0.0))
                kt = kc * jnp.exp(jnp.minimum(b_m - cum, 0.0))
                a_lvl = _dot_nt(qt.astype(BF16), kt.astype(BF16))
                grp_r = tau_r >> lvl
                mask = jnp.logical_and((grp_r & 1) == 1, (tau_c >> lvl) == grp_r - 1)
                a_mat = a_mat + jnp.where(mask, a_lvl, 0.0)
                if lvl < 5:
                    nxt = pltpu.roll(b_m, m if rev else (CHUNK - m), axis=0)
                    b_m = jnp.where(((tau_row >> lvl) & 1) == 1, b_m, nxt)

            st = st_scr[...]
            o_c = (jnp.dot(a_mat.astype(BF16), vc.astype(BF16), preferred_element_type=F32)
                   + _dot_nt((qs * jnp.exp(cum)).astype(BF16), st.astype(BF16)))
            kd = kc * jnp.exp(total - cum)
            st_scr[...] = st * jnp.exp(total) + _dot_tn(vc.astype(BF16), kd.astype(BF16))

            if not rev:
                ofwd_scr[rows, :] = o_c
            else:
                tot = ofwd_scr[rows, :] + o_c
                y = tot * lax.rsqrt(jnp.mean(tot * tot, axis=-1, keepdims=True) + EPS) * nrm
                o_ref[0, rows, :] = (y * _silu(g_ref[0, rows, :])).astype(o_ref.dtype)
            return carry

        lax.fori_loop(0, n_chunks, chunk_body, 0)
        if want_state:
            sfin_ref[0, d, 0] = st_scr[...].T


def gla_pallas(proj, lb, hg_norm, s0, want_state):
    bsz, seq_len, _ = proj.shape
    n_chunks = seq_len // CHUNK
    has_s0 = s0 is not None

    def col(group):
        return pl.BlockSpec((1, seq_len, LANES), lambda b, h, group=group: (b, 0, group * HG_HEADS + h))

    in_specs = [col(0), col(1), col(2), col(3), col(4),
                pl.BlockSpec((2, LANES), lambda b, h: (0, h)),
                pl.BlockSpec((1, LANES), lambda b, h: (0, 0))]
    args = [proj, proj, proj, proj, proj, lb, hg_norm.reshape(1, LANES)]
    if has_s0:
        in_specs.append(pl.BlockSpec((1, 2, 1, HG_DK, HG_DV), lambda b, h: (b, 0, h, 0, 0)))
        args.append(s0)
    out_specs = [pl.BlockSpec((1, seq_len, LANES), lambda b, h: (b, 0, h))]
    out_shape = [jax.ShapeDtypeStruct((bsz, seq_len, MIX_W), BF16)]
    if want_state:
        out_specs.append(pl.BlockSpec((1, 2, 1, HG_DK, HG_DV), lambda b, h: (b, 0, h, 0, 0)))
        out_shape.append(jax.ShapeDtypeStruct((bsz, 2, HG_HEADS, HG_DK, HG_DV), F32))
    res = pl.pallas_call(
        functools.partial(_gla_kernel, n_chunks=n_chunks, has_s0=has_s0, want_state=want_state),
        grid=(bsz, HG_HEADS),
        in_specs=in_specs, out_specs=out_specs, out_shape=out_shape,
        scratch_shapes=[pltpu.VMEM((seq_len, LANES), F32), pltpu.VMEM((HG_DV, HG_DK), F32)],
        compiler_params=_cparams("arbitrary", "arbitrary"),
        name="hgrn2_scan",
    )(*args)
    return (res[0], res[1]) if want_state else (res[0], None)


def _swap_halves64(x):
    lane = lax.broadcasted_iota(jnp.int32, x.shape, 1)
    return jnp.where((lane & 63) < 32, pltpu.roll(x, 96, axis=1), pltpu.roll(x, 32, axis=1))


def _mla_kernel(*refs, seq_len, past, latent):
    cq_ref, ckv_ref, kpe_ref, g_ref, qn_ref, kvn_ref, wuq_ref, wukv_ref = refs[:8]
    pos = 8
    if latent:
        ctx_ckv_ref, ctx_kpe_ref, cosk_ref, sink_ref, cosq_ref, sinq_ref = refs[pos:pos + 6]
        pos += 6
    o_ref = refs[pos]
    pos += 1
    if not latent:
        ckv_out_ref, kpe_out_ref = refs[pos:pos + 2]
        pos += 2
    kn_scr, v_scr, kpe2_scr = refs[pos:pos + 3]

    @pl.when(pl.program_id(1) == 0)
    def _build_keys():
        wukv = wukv_ref[...]

        def put(rows_bf16, base, n):
            kv = jnp.dot(rows_bf16, wukv, preferred_element_type=F32)
            for h in range(MLA_HEADS):
                kn_scr[h, base:base + n, :] = kv[:, h * 256:h * 256 + 128].astype(BF16)
                v_scr[h, base:base + n, :] = kv[:, h * 256 + 128:(h + 1) * 256].astype(BF16)

        if latent:
            put(ctx_ckv_ref[0, 0].astype(BF16), 0, past)
            kpe2_scr[0:past, :] = ctx_kpe_ref[0, 0].astype(BF16)
        kvn = kvn_ref[...]
        for r0 in range(0, seq_len, KV_ROWS):
            n = min(KV_ROWS, seq_len - r0)
            c = ckv_ref[0, r0:r0 + n, :]
            cn = c * lax.rsqrt(jnp.mean(c * c, axis=-1, keepdims=True) + EPS) * kvn
            if not latent:
                ckv_out_ref[0, r0:r0 + n, :] = cn
            put(cn.astype(BF16), past + r0, n)
        kp = kpe_ref[0]
        if latent:
            kp = kp * cosk_ref[...] + _swap_halves64(kp) * sink_ref[...]
        else:
            kpe_out_ref[0] = kp[:, :MLA_ROPE]
        kpe2_scr[past:past + seq_len, :] = (kp + pltpu.roll(kp, MLA_ROPE, axis=1)).astype(BF16)

    cq = cq_ref[0]
    cqn = cq * lax.rsqrt(jnp.mean(cq * cq, axis=-1, keepdims=True) + EPS) * qn_ref[...]
    q = jnp.dot(cqn.astype(BF16), wuq_ref[...], preferred_element_type=F32)
    lane = lax.broadcasted_iota(jnp.int32, (q.shape[0], LANES), 1)
    kpe2 = kpe2_scr[...]
    for h in range(MLA_HEADS):
        qn = q[:, h * 128:(h + 1) * 128]
        p0 = MLA_HEADS * MLA_NOPE + (h // 2) * LANES
        qp = q[:, p0:p0 + LANES]
        if latent:
            qp = qp * cosq_ref[...] + _swap_halves64(qp) * sinq_ref[...]
        qp = jnp.where((lane >> 6) == (h % 2), qp, 0.0)
        s = (_dot_nt(qn.astype(BF16), kn_scr[h]) + _dot_nt(qp.astype(BF16), kpe2)) * MLA_SCALE
        e = jnp.exp(s - jnp.max(s, axis=-1, keepdims=True))
        p = e / jnp.sum(e, axis=-1, keepdims=True)
        o_h = jnp.dot(p.astype(BF16), v_scr[h], preferred_element_type=F32)
        o_ref[0, :, h * 128:(h + 1) * 128] = (o_h * _silu(g_ref[0, :, h * 128:(h + 1) * 128])).astype(o_ref.dtype)


def mla_pallas(proj, q_norm, kv_norm, w_uq_p, w_ukv, ctx_ckv=None, ctx_kpe2=None, e_idx=0, rope=None):
    bsz, seq_len, _ = proj.shape
    latent = ctx_ckv is not None
    past = ctx_ckv.shape[2] if latent else 0
    lk = past + seq_len
    tq = min(MLA_TQ, seq_len)
    nq = seq_len // tq
    in_specs = [
        pl.BlockSpec((1, tq, 512), lambda b, i: (b, i, 10)),
        pl.BlockSpec((1, seq_len, 512), lambda b, i: (b, 0, 11)),
        pl.BlockSpec((1, seq_len, LANES), lambda b, i: (b, 0, 56)),
        pl.BlockSpec((1, tq, 1024), lambda b, i: (b, i, 6)),
        pl.BlockSpec((1, 512), lambda b, i: (0, 0)),
        pl.BlockSpec((1, 512), lambda b, i: (0, 0)),
        pl.BlockSpec((512, 1536), lambda b, i: (0, 0)),
        pl.BlockSpec((512, 2048), lambda b, i: (0, 0)),
    ]
    args = [proj, proj, proj, proj, q_norm.reshape(1, 512), kv_norm.reshape(1, 512), w_uq_p, w_ukv]
    if latent:
        cos_t, sin_t = rope
        in_specs += [
            pl.BlockSpec((1, 1, past, 512), lambda b, i: (b, e_idx, 0, 0)),
            pl.BlockSpec((1, 1, past, LANES), lambda b, i: (b, e_idx, 0, 0)),
            pl.BlockSpec((seq_len, LANES), lambda b, i: (0, 0)),
            pl.BlockSpec((seq_len, LANES), lambda b, i: (0, 0)),
            pl.BlockSpec((tq, LANES), lambda b, i: (i, 0)),
            pl.BlockSpec((tq, LANES), lambda b, i: (i, 0)),
        ]
        args += [ctx_ckv, ctx_kpe2, cos_t, sin_t, cos_t, sin_t]
    out_specs = [pl.BlockSpec((1, tq, 1024), lambda b, i: (b, i, 0))]
    out_shape = [jax.ShapeDtypeStruct((bsz, seq_len, MIX_W), BF16)]
    if not latent:
        out_specs += [pl.BlockSpec((1, seq_len, 512), lambda b, i: (b, 0, 0)),
                      pl.BlockSpec((1, seq_len, MLA_ROPE), lambda b, i: (b, 0, 0))]
        out_shape += [jax.ShapeDtypeStruct((bsz, seq_len, MLA_KV_RANK), F32),
                      jax.ShapeDtypeStruct((bsz, seq_len, MLA_ROPE), F32)]
    return pl.pallas_call(
        functools.partial(_mla_kernel, seq_len=seq_len, past=past, latent=latent),
        grid=(bsz, nq),
        in_specs=in_specs, out_specs=out_specs, out_shape=out_shape,
        scratch_shapes=[pltpu.VMEM((MLA_HEADS, lk, LANES), BF16), pltpu.VMEM((MLA_HEADS, lk, LANES), BF16),
                        pltpu.VMEM((lk, LANES), BF16)],
        compiler_params=_cparams("arbitrary", "arbitrary"),
        name="mla_attention",
    )(*args)


def axial_rope(n_tokens):
    rows = n_tokens // GRID_W
    row = jnp.repeat(jnp.arange(rows, dtype=jnp.float32), GRID_W)
    col = jnp.arange(n_tokens, dtype=jnp.int32) % GRID_W
    half = MLA_ROPE // 2
    inv_freq = ROPE_BASE ** (-jnp.arange(0, half, 2, dtype=jnp.float32) / half)
    ang = jnp.concatenate([row[:, None] * inv_freq, col.astype(jnp.float32)[:, None] * inv_freq], axis=-1)
    return jnp.cos(ang), jnp.sin(ang)


def rope_tables(cos, sin):
    return jnp.tile(cos, (1, 4)), jnp.tile(jnp.concatenate([-sin, sin], axis=-1), (1, 2))


def permute_even_in(w):
    return jnp.concatenate([w[:, :6144], w[:, 6208:7232], w[:, 6144:6208],
                            jnp.zeros((w.shape[0], EVEN_IN_PAD - EVEN_IN), w.dtype)], axis=1)


def permute_w_uq(w):
    w3 = w.reshape(w.shape[0], MLA_HEADS, MLA_NOPE + MLA_ROPE)
    return jnp.concatenate([w3[:, :, :MLA_NOPE].reshape(w.shape[0], -1),
                            w3[:, :, MLA_NOPE:].reshape(w.shape[0], -1)], axis=1)


def _gdn_kernel(*refs, seq_len, has_s0, want_state):
    q_ref, k_ref, v_ref, z_ref, ab_ref, wq_ref, wk_ref, wv_ref, alog_ref, dt_ref, nrm_ref = refs[:11]
    pos = 11
    s0_ref = None
    if has_s0:
        s0_ref = refs[pos]
        pos += 1
    o_ref = refs[pos]
    pos += 1
    sfin_ref = None
    if want_state:
        sfin_ref = refs[pos]
        pos += 1
    qn_scr, kn_scr, vn_scr, gb_scr, ofwd_scr, s_scr = refs[pos:pos + 6]
    n_chunks = seq_len // CHUNK
    head = pl.program_id(1)

    row = lax.broadcasted_iota(jnp.int32, (seq_len, LANES), 0)

    def conv_silu(x_ref, w_ref):
        x = x_ref[0]
        w = w_ref[...]
        x_prev = jnp.where(row == 0, 0.0, pltpu.roll(x, 1, axis=0))
        x_next = jnp.where(row == seq_len - 1, 0.0, pltpu.roll(x, seq_len - 1, axis=0))
        return _silu(x_prev * w[0:1, :] + x * w[1:2, :] + x_next * w[2:3, :])

    def l2n(x):
        return x * lax.rsqrt(jnp.sum(x * x, axis=-1, keepdims=True) + EPS)

    qn_scr[...] = l2n(conv_silu(q_ref, wq_ref)) * GDN_DK ** -0.5
    kn_scr[...] = l2n(conv_silu(k_ref, wk_ref))
    vn_scr[...] = conv_silu(v_ref, wv_ref)
    ab = ab_ref[0]
    lane_full = lax.broadcasted_iota(jnp.int32, (seq_len, LANES), 1)
    pre = ab + dt_ref[...]
    softplus = jnp.maximum(pre, 0.0) + jnp.log1p(jnp.exp(-jnp.abs(pre)))
    gb_scr[...] = jnp.where(lane_full < 2 * GDN_HEADS, -jnp.exp(alog_ref[...]) * softplus, jax.nn.sigmoid(ab))

    r64 = lax.broadcasted_iota(jnp.int32, (CHUNK, CHUNK), 0)
    c64 = lax.broadcasted_iota(jnp.int32, (CHUNK, CHUNK), 1)
    lane = lax.broadcasted_iota(jnp.int32, (CHUNK, LANES), 1)
    eye = r64 == c64
    eye_f = eye.astype(F32)
    nrm = nrm_ref[...]

    for d in (0, 1):
        rev = d == 1
        tau_r = (CHUNK - 1 - r64) if rev else r64
        tau_c = (CHUNK - 1 - c64) if rev else c64
        incl = tau_c <= tau_r
        strict = tau_c < tau_r
        tri = incl.astype(F32)
        g_lane = d * GDN_HEADS + head
        b_lane = 2 * GDN_HEADS + d * GDN_HEADS + head
        if has_s0:
            s_scr[...] = s0_ref[0, d, 0]
        else:
            s_scr[...] = jnp.zeros_like(s_scr)

        def chunk_body(ci, carry, rev=rev, tau_r=tau_r, tau_c=tau_c, incl=incl, strict=strict, tri=tri,
                       g_lane=g_lane, b_lane=b_lane):
            cidx = (n_chunks - 1 - ci) if rev else ci
            rows = pl.ds(pl.multiple_of(cidx * CHUNK, CHUNK), CHUNK)
            qc, kc, vc, gb = qn_scr[rows, :], kn_scr[rows, :], vn_scr[rows, :], gb_scr[rows, :]
            cum_all = jnp.dot(tri, gb, precision=HIGHEST, preferred_element_type=F32)
            cum_col = jnp.sum(jnp.where(lane == g_lane, cum_all, 0.0), axis=-1, keepdims=True)
            beta_col = jnp.sum(jnp.where(lane == b_lane, gb, 0.0), axis=-1, keepdims=True)
            cum_row = jnp.sum(jnp.where(eye, cum_col, 0.0), axis=0, keepdims=True)
            total = cum_col[0:1, :] if rev else cum_col[CHUNK - 1:CHUNK, :]
            decay = jnp.exp(jnp.minimum(cum_col - cum_row, 0.0))
            k_beta = kc * beta_col
            kk = _dot_nt(k_beta.astype(BF16), kc.astype(BF16))
            lower = jnp.where(strict, kk * decay, 0.0)
            t_mat = eye_f - jnp.where(jnp.logical_and((tau_r & 1) == 1, tau_c == tau_r - 1), lower, 0.0)
            for lvl in range(1, 6):
                grp_r = tau_r >> lvl
                mask = jnp.logical_and((grp_r & 1) == 1, (tau_c >> lvl) == grp_r - 1)
                c_m = jnp.where(mask, lower, 0.0)
                t_mat = t_mat - _dot3(_dot3(t_mat, c_m), t_mat)
            t_bf = t_mat.astype(BF16)
            e_cum = jnp.exp(cum_col)
            w = jnp.dot(t_bf, (k_beta * e_cum).astype(BF16), preferred_element_type=F32)
            u = jnp.dot(t_bf, (vc * beta_col).astype(BF16), preferred_element_type=F32)
            s_bf = s_scr[...].astype(BF16)
            v_new = u - jnp.dot(w.astype(BF16), s_bf, preferred_element_type=F32)
            attn = jnp.where(incl, _dot_nt(qc.astype(BF16), kc.astype(BF16)) * decay, 0.0)
            o_c = (jnp.dot((qc * e_cum).astype(BF16), s_bf, preferred_element_type=F32)
                   + jnp.dot(attn.astype(BF16), v_new.astype(BF16), preferred_element_type=F32))
            kd = kc * jnp.exp(total - cum_col)
            s_scr[...] = jnp.exp(total) * s_scr[...] + _dot_tn(kd.astype(BF16), v_new.astype(BF16))
            if not rev:
                ofwd_scr[rows, :] = o_c
            else:
                tot = ofwd_scr[rows, :] + o_c
                y = tot * lax.rsqrt(jnp.mean(tot * tot, axis=-1, keepdims=True) + EPS) * nrm
                o_ref[0, rows, :] = (y * _silu(z_ref[0, rows, :])).astype(o_ref.dtype)
            return carry

        lax.fori_loop(0, n_chunks, chunk_body, 0)
        if want_state:
            sfin_ref[0, d, 0] = s_scr[...]


def gdn_pallas(proj, conv_w, a_log, dt_bias, gdn_norm, s0, want_state):
    bsz, seq_len, _ = proj.shape
    has_s0 = s0 is not None

    def col(group):
        return pl.BlockSpec((1, seq_len, LANES), lambda b, h, group=group: (b, 0, group * GDN_HEADS + h))

    def wcol(group):
        return pl.BlockSpec((CONV_W, LANES), lambda b, h, group=group: (0, group * GDN_HEADS + h))

    vec = pl.BlockSpec((1, LANES), lambda b, h: (0, 0))
    pad = jnp.zeros((LANES - 2 * GDN_HEADS,), F32)
    alog_vec = jnp.concatenate([a_log.reshape(-1), pad]).reshape(1, LANES)
    dt_vec = jnp.concatenate([dt_bias.reshape(-1), pad]).reshape(1, LANES)
    gate_block = (3 * MIX_W + MIX_W + 3 * HY_CH + HY_CH) // LANES
    in_specs = [col(0), col(1), col(2), col(3),
                pl.BlockSpec((1, seq_len, LANES), lambda b, h: (b, 0, gate_block)),
                wcol(0), wcol(1), wcol(2), vec, vec, vec]
    args = [proj, proj, proj, proj, proj, conv_w, conv_w, conv_w, alog_vec, dt_vec, gdn_norm.reshape(1, LANES)]
    if has_s0:
        in_specs.append(pl.BlockSpec((1, 2, 1, GDN_DK, GDN_DV), lambda b, h: (b, 0, h, 0, 0)))
        args.append(s0)
    out_specs = [pl.BlockSpec((1, seq_len, LANES), lambda b, h: (b, 0, h))]
    out_shape = [jax.ShapeDtypeStruct((bsz, seq_len, MIX_W), BF16)]
    if want_state:
        out_specs.append(pl.BlockSpec((1, 2, 1, GDN_DK, GDN_DV), lambda b, h: (b, 0, h, 0, 0)))
        out_shape.append(jax.ShapeDtypeStruct((bsz, 2, GDN_HEADS, GDN_DK, GDN_DV), F32))
    seq_scr = pltpu.VMEM((seq_len, LANES), F32)
    res = pl.pallas_call(
        functools.partial(_gdn_kernel, seq_len=seq_len, has_s0=has_s0, want_state=want_state),
        grid=(bsz, GDN_HEADS),
        in_specs=in_specs, out_specs=out_specs, out_shape=out_shape,
        scratch_shapes=[seq_scr, seq_scr, seq_scr, seq_scr, seq_scr, pltpu.VMEM((GDN_DK, GDN_DV), F32)],
        compiler_params=_cparams("arbitrary", "arbitrary"),
        name="gdn_scan",
    )(*args)
    return (res[0], res[1]) if want_state else (res[0], None)


def permute_odd_in(w):
    return jnp.concatenate([w[:, :3072], w[:, 3104:4128], w[:, 4128:7200], w[:, 7200:8224], w[:, 3072:3104],
                            jnp.zeros((w.shape[0], ODD_IN_PAD - ODD_IN), w.dtype)], axis=1)


def dwconv_centred(x, w, b=None):
    pad = CONV_W // 2
    y = lax.conv_general_dilated(x, w.astype(x.dtype)[:, None, :], window_strides=(1,),
                                 padding=[(pad, pad)], dimension_numbers=("NWC", "WIO", "NWC"),
                                 feature_group_count=x.shape[-1])
    return y if b is None else y + b.astype(x.dtype)


def hyena_filter(n, w1, b1, w2, b2, w3):
    f32 = jnp.float32
    t = jnp.linspace(0.0, 1.0, n, dtype=f32)[:, None]
    w = 2.0 * math.pi * jnp.arange(n, dtype=f32)[:, None] / n
    f = jnp.linspace(1e-4, HY_BANDS - 1, HY_BANDS, dtype=f32)[None, :]
    z = jnp.concatenate([t, jnp.cos(f * w), -jnp.sin(f * w)], axis=-1)
    hid = jnp.sin(z @ w1.astype(f32) + b1.astype(f32))
    hid = jnp.sin(hid @ w2.astype(f32) + b2.astype(f32))
    h = hid @ w3.astype(f32)
    max_decay = math.log(HY_TARGET) / HY_DECAY_PCT_SHORT
    min_decay = math.log(HY_TARGET) / HY_DECAY_PCT_LONG
    deltas = jnp.abs(jnp.linspace(min_decay, max_decay, HY_CH, dtype=f32))
    deltas = jnp.concatenate([deltas, deltas])
    h = h * (jnp.exp(-t * deltas) + HY_SHIFT)
    h_fwd, h_bwd = jnp.split(h, 2, axis=-1)
    h_full = jnp.concatenate([h_fwd, jnp.zeros((1, HY_CH), f32), h_bwd[:0:-1]], axis=0)
    return h_full / jnp.sum(jnp.abs(h_full), axis=0, keepdims=True)


def bidir_fftconv(u, h_full, skip):
    n = u.shape[1]
    uf = u.astype(jnp.float32)
    spec = jnp.fft.rfft(uf, n=2 * n, axis=1) * jnp.fft.rfft(h_full, n=2 * n, axis=0)[None]
    y = jnp.fft.irfft(spec, n=2 * n, axis=1)[:, :n]
    return (y + uf * skip.astype(jnp.float32)).astype(u.dtype)


def hyena_jnp(proj, hy_conv_w, hy_conv_b, hy_w1, hy_b1, hy_w2, hy_b2, hy_w3, hy_skip):
    seq_len = proj.shape[1]
    hy_in = proj[..., 4 * MIX_W:4 * MIX_W + 3 * HY_CH]
    hy_g = proj[..., 4 * MIX_W + 3 * HY_CH:4 * MIX_W + 4 * HY_CH]
    v_h, x0, x1 = jnp.split(dwconv_centred(hy_in, hy_conv_w, hy_conv_b), 3, axis=-1)
    filt = hyena_filter(seq_len, hy_w1, hy_b1, hy_w2, hy_b2, hy_w3)
    o_hy = x0 * bidir_fftconv(v_h * x1, filt, hy_skip) * jax.nn.silu(hy_g)
    return o_hy.astype(BF16)


def kernel(x_prompt, x_sample, state_hgrn, cache_mla_ckv, cache_mla_kpe, state_gdn, c, c_ctx,
           ada_w, ada_b, norm_pre, norm_post, ev_in_w, ev_out_w, hg_lb_logits, hg_norm,
           mla_q_norm, mla_w_uq, mla_kv_norm, mla_w_ukv, od_in_w, od_out_w, gdn_conv, gdn_a_log,
           gdn_dt_bias, gdn_norm, hy_conv_w, hy_conv_b, hy_w1, hy_b1, hy_w2, hy_b2, hy_w3, hy_skip):
    bp, lp, d = x_prompt.shape
    bs, ls, _ = x_sample.shape

    lb_cum = jnp.cumsum(jax.nn.softmax(hg_lb_logits.astype(jnp.float32), axis=1), axis=1)
    hg_lb = lb_cum - lb_cum[:, :1]

    cond = jnp.concatenate([c_ctx[None], c, jnp.zeros((8 - 1 - bs, d), c.dtype)], axis=0)
    mods = adaln_all(cond, ada_w, ada_b)

    rope = rope_tables(*axial_rope(ls))
    ctx_kpe2 = jnp.concatenate([cache_mla_kpe, cache_mla_kpe], axis=-1)

    xp = x_prompt.reshape(bp * lp, d)
    xs = x_sample.reshape(bs * ls, d)
    new_hg, new_ckv, new_kpe, new_gdn = [], [], [], []
    for l in range(DEPTH):
        shift, scale, gate = jnp.split(mods[l], 3, axis=-1)
        shift_p, scale_p, gate_p = (t[0:1, None, :] for t in (shift, scale, gate))
        shift_s, scale_s, gate_s = (t[1:1 + bs, None, :] for t in (shift, scale, gate))
        gain_pre = norm_pre[l][None]
        gain_post = norm_post[l][None]
        if l % 2 == 0:
            e = l // 2
            w_in = permute_even_in(ev_in_w[e]).astype(BF16)
            w_out = ev_out_w[e].astype(BF16)
            tn = EVEN_TN
        else:
            o = l // 2
            w_in = permute_odd_in(od_in_w[o]).astype(BF16)
            w_out = od_out_w[o].astype(BF16)
            tn = ODD_TN
        proj_p = in_proj(xp, gain_pre, scale_p, shift_p, w_in, tn, bp * lp).reshape(bp, lp, -1)
        proj_s = in_proj(xs, gain_pre, scale_s, shift_s, w_in, tn, ls).reshape(bs, ls, -1)
        if l % 2 == 0:
            lb = hg_lb[:, l]
            w_uq = permute_w_uq(mla_w_uq[e]).astype(BF16)
            w_ukv = mla_w_ukv[e].astype(BF16)
            oa_p, s_hg = gla_pallas(proj_p, lb, hg_norm[e], None, True)
            oa_s, _ = gla_pallas(proj_s, lb, hg_norm[e], state_hgrn[:, e], False)
            ob_p, ckv, kpe = mla_pallas(proj_p, mla_q_norm[e], mla_kv_norm[e], w_uq, w_ukv)
            ob_s, = mla_pallas(proj_s, mla_q_norm[e], mla_kv_norm[e], w_uq, w_ukv,
                               cache_mla_ckv, ctx_kpe2, e, rope)
            new_hg.append(s_hg)
            new_ckv.append(ckv)
            new_kpe.append(kpe)
        else:
            oa_p, s_gdn = gdn_pallas(proj_p, gdn_conv[o], gdn_a_log[o], gdn_dt_bias[o], gdn_norm[o], None, True)
            oa_s, _ = gdn_pallas(proj_s, gdn_conv[o], gdn_a_log[o], gdn_dt_bias[o], gdn_norm[o],
                                 state_gdn[:, o], False)
            hw = (hy_conv_w[o], hy_conv_b[o], hy_w1[o], hy_b1[o], hy_w2[o], hy_b2[o], hy_w3[o], hy_skip[o])
            ob_p = hyena_jnp(proj_p, *hw)
            ob_s = hyena_jnp(proj_s, *hw)
            new_gdn.append(s_gdn)
        xp = out_proj(oa_p.reshape(bp * lp, -1), ob_p.reshape(bp * lp, -1), w_out, gain_post, gate_p, xp, bp * lp)
        xs = out_proj(oa_s.reshape(bs * ls, -1), ob_s.reshape(bs * ls, -1), w_out, gain_post, gate_s, xs, ls)

    return (xp.reshape(bp, lp, d), xs.reshape(bs, ls, d), jnp.stack(new_hg, axis=1),
            jnp.stack(new_ckv, axis=1), jnp.stack(new_kpe, axis=1), jnp.stack(new_gdn, axis=1))
```

```python
import functools
import math

import jax
import jax.numpy as jnp
import numpy as np
from jax import lax
from jax.experimental import pallas as pl
from jax.experimental.pallas import tpu as pltpu

D_MODEL = 2048
DEPTH = 4
GRID_W = 64
N_EVEN = (DEPTH + 1) // 2
N_ODD = DEPTH // 2
MIX_W = D_MODEL // 2
HG_HEADS = 8
HG_DK = MIX_W // HG_HEADS
HG_DV = MIX_W // HG_HEADS
MLA_HEADS = 8
MLA_NOPE = 128
MLA_ROPE = 64
MLA_V = MIX_W // MLA_HEADS
MLA_Q_RANK = D_MODEL // 4
MLA_KV_RANK = D_MODEL // 4
MLA_SCALE = (MLA_NOPE + MLA_ROPE) ** -0.5
GDN_HEADS = 8
GDN_DK = MIX_W // GDN_HEADS
GDN_DV = MIX_W // GDN_HEADS
HY_CH = MIX_W
HY_EMB = 33
HY_BANDS = (HY_EMB - 1) // 2
HY_HIDDEN = 64
HY_DECAY_PCT_SHORT = 0.3
HY_DECAY_PCT_LONG = 1.5
HY_TARGET = 1e-2
HY_SHIFT = 0.05
CONV_W = 3
CHUNK = 64
ROPE_BASE = 10000.0
EPS = 1e-6

EVEN_IN = 5 * MIX_W + MLA_Q_RANK + MLA_KV_RANK + MLA_ROPE + MIX_W
ODD_IN = 3 * MIX_W + 4 * GDN_HEADS + MIX_W + 3 * HY_CH + HY_CH

LANES = 128
VMEM_LIMIT_BYTES = 56 * 1024 * 1024

TOKEN_TILE = 256
EVEN_IN_PAD = 3 * 19 * LANES
EVEN_TN = 19 * LANES
ODD_IN_PAD = 5 * 13 * LANES
ODD_TN = 13 * LANES
MLA_TQ = 256
KV_ROWS = 512
HY_TC = 256
HY_KB = 256

F32 = jnp.float32
BF16 = jnp.bfloat16
HIGHEST = lax.Precision.HIGHEST


def _cparams(*sem):
    return pltpu.CompilerParams(dimension_semantics=sem, vmem_limit_bytes=VMEM_LIMIT_BYTES)


def _dot_nt(a, b):
    return lax.dot_general(a, b, (((1,), (1,)), ((), ())), preferred_element_type=F32)


def _dot_tn(a, b):
    return lax.dot_general(a, b, (((0,), (0,)), ((), ())), preferred_element_type=F32)


def _silu(x):
    return x * jax.nn.sigmoid(x)


def _split_bf16(a):
    hi = a.astype(BF16)
    return hi, (a - hi.astype(F32)).astype(BF16)


def _dot3(a, b):
    a_hi, a_lo = _split_bf16(a)
    b_hi, b_lo = _split_bf16(b)
    return (jnp.dot(a_hi, b_hi, preferred_element_type=F32) + jnp.dot(a_hi, b_lo, preferred_element_type=F32)
            + jnp.dot(a_lo, b_hi, preferred_element_type=F32))


def _adaln_kernel(cond_ref, w_ref, b_ref, o_ref):
    act = _silu(cond_ref[...]).astype(BF16)
    o_ref[0] = jnp.dot(act, w_ref[0].astype(BF16), preferred_element_type=F32) + b_ref[0]


def adaln_all(cond, ada_w, ada_b):
    rows = cond.shape[0]
    tn = 1024
    n3 = ada_w.shape[-1]
    return pl.pallas_call(
        _adaln_kernel,
        grid=(ada_w.shape[0], n3 // tn),
        in_specs=[
            pl.BlockSpec((rows, D_MODEL), lambda l, j: (0, 0)),
            pl.BlockSpec((1, D_MODEL, tn), lambda l, j: (l, 0, j)),
            pl.BlockSpec((1, 1, tn), lambda l, j: (l, 0, j)),
        ],
        out_specs=pl.BlockSpec((1, rows, tn), lambda l, j: (l, 0, j)),
        out_shape=jax.ShapeDtypeStruct((ada_w.shape[0], rows, n3), F32),
        compiler_params=_cparams("arbitrary", "arbitrary"),
        name="adaln_mod",
    )(cond, ada_w, ada_b.reshape(ada_b.shape[0], 1, n3))


def _in_proj_kernel(x_ref, gain_ref, scale_ref, shift_ref, w_ref, o_ref):
    x = x_ref[...]
    y = x * lax.rsqrt(jnp.mean(x * x, axis=-1, keepdims=True) + EPS)
    y = y * gain_ref[...]
    h = y * (1.0 + scale_ref[0]) + shift_ref[0]
    o_ref[...] = jnp.dot(h.astype(BF16), w_ref[...], preferred_element_type=F32)


def in_proj(x2d, gain, scale, shift, w_bf16, tn, tokens_per_mod_row):
    t, d = x2d.shape
    n = w_bf16.shape[1]
    tiles_per_row = tokens_per_mod_row // TOKEN_TILE
    return pl.pallas_call(
        _in_proj_kernel,
        grid=(n // tn, t // TOKEN_TILE),
        in_specs=[
            pl.BlockSpec((TOKEN_TILE, d), lambda j, i: (i, 0)),
            pl.BlockSpec((1, d), lambda j, i: (0, 0)),
            pl.BlockSpec((1, 1, d), lambda j, i: (i // tiles_per_row, 0, 0)),
            pl.BlockSpec((1, 1, d), lambda j, i: (i // tiles_per_row, 0, 0)),
            pl.BlockSpec((d, tn), lambda j, i: (0, j)),
        ],
        out_specs=pl.BlockSpec((TOKEN_TILE, tn), lambda j, i: (i, j)),
        out_shape=jax.ShapeDtypeStruct((t, n), F32),
        compiler_params=_cparams("arbitrary", "arbitrary"),
        name="in_proj",
    )(x2d, gain, scale, shift, w_bf16)


def _out_proj_kernel(oa_ref, ob_ref, wa_ref, wb_ref, gain_ref, gate_ref, x_ref, y_ref):
    acc = (jnp.dot(oa_ref[...], wa_ref[...], preferred_element_type=F32)
           + jnp.dot(ob_ref[...], wb_ref[...], preferred_element_type=F32))
    nrm = acc * lax.rsqrt(jnp.mean(acc * acc, axis=-1, keepdims=True) + EPS) * gain_ref[...]
    y_ref[...] = x_ref[...] + gate_ref[0] * nrm


def out_proj(oa, ob, w_bf16, gain, gate, x2d, tokens_per_mod_row):
    t, d = x2d.shape
    tiles_per_row = tokens_per_mod_row // TOKEN_TILE
    return pl.pallas_call(
        _out_proj_kernel,
        grid=(t // TOKEN_TILE,),
        in_specs=[
            pl.BlockSpec((TOKEN_TILE, MIX_W), lambda i: (i, 0)),
            pl.BlockSpec((TOKEN_TILE, MIX_W), lambda i: (i, 0)),
            pl.BlockSpec((MIX_W, d), lambda i: (0, 0)),
            pl.BlockSpec((MIX_W, d), lambda i: (1, 0)),
            pl.BlockSpec((1, d), lambda i: (0, 0)),
            pl.BlockSpec((1, 1, d), lambda i: (i // tiles_per_row, 0, 0)),
            pl.BlockSpec((TOKEN_TILE, d), lambda i: (i, 0)),
        ],
        out_specs=pl.BlockSpec((TOKEN_TILE, d), lambda i: (i, 0)),
        out_shape=jax.ShapeDtypeStruct((t, d), F32),
        compiler_params=_cparams("arbitrary"),
        name="out_proj",
    )(oa, ob, w_bf16, w_bf16, gain, gate, x2d)


def _gla_kernel(*refs, n_chunks, has_s0, want_state):
    q_ref, ffw_ref, fbw_ref, v_ref, g_ref, lb_ref, nrm_ref = refs[:7]
    pos = 7
    s0_ref = None
    if has_s0:
        s0_ref = refs[pos]
        pos += 1
    o_ref = refs[pos]
    pos += 1
    sfin_ref = None
    if want_state:
        sfin_ref = refs[pos]
        pos += 1
    ofwd_scr, st_scr = refs[pos], refs[pos + 1]

    r64 = lax.broadcasted_iota(jnp.int32, (CHUNK, CHUNK), 0)
    c64 = lax.broadcasted_iota(jnp.int32, (CHUNK, CHUNK), 1)
    row128 = lax.broadcasted_iota(jnp.int32, (CHUNK, LANES), 0)
    nrm = nrm_ref[...]

    for d in (0, 1):
        rev = d == 1
        f_ref = fbw_ref if rev else ffw_ref
        lbd = lb_ref[d:d + 1, :]
        log_lb = jnp.log(lbd)
        log_1m = jnp.log1p(-lbd)
        tau_r = (CHUNK - 1 - r64) if rev else r64
        tau_c = (CHUNK - 1 - c64) if rev else c64
        tau_row = (CHUNK - 1 - row128) if rev else row128
        tri = (tau_c <= tau_r).astype(F32)
        eye = r64 == c64
        if has_s0:
            st_scr[...] = s0_ref[0, d, 0].T
        else:
            st_scr[...] = jnp.zeros_like(st_scr)

        def chunk_body(ci, carry, rev=rev, f_ref=f_ref, log_lb=log_lb, log_1m=log_1m,
                       tau_r=tau_r, tau_c=tau_c, tau_row=tau_row, tri=tri, eye=eye):
            cidx = (n_chunks - 1 - ci) if rev else ci
            rows = pl.ds(pl.multiple_of(cidx * CHUNK, CHUNK), CHUNK)
            qs = _silu(q_ref[0, rows, :]) * HG_DK ** -0.5
            vc = v_ref[0, rows, :]
            f_raw = f_ref[0, rows, :]
            log_sig = jnp.minimum(f_raw, 0.0) - jnp.log1p(jnp.exp(-jnp.abs(f_raw)))
            t = log_1m + log_sig
            log_f = jnp.maximum(log_lb, t) + jnp.log1p(jnp.exp(-jnp.abs(log_lb - t)))
            kc = 1.0 - jnp.exp(log_f)
            cum = jnp.dot(tri, log_f, precision=HIGHEST, preferred_element_type=F32)
            total = cum[0:1, :] if rev else cum[CHUNK - 1:CHUNK, :]

            a_mat = jnp.where(eye, jnp.sum(qs * kc, axis=-1, keepdims=True), 0.0)
            b_m = cum
            for lvl in range(6):
                m = 1 << lvl
                prev_b = pltpu.roll(b_m, (CHUNK - m) if rev else m, axis=0)
                qt = qs * jnp.exp(jnp.minimum(cum - prev_b, 0.0))
                kt = kc * jnp.exp(jnp.minimum(b_m - cum, 0.0))
                a_lvl = _dot_nt(qt.astype(BF16), kt.astype(BF16))
                grp_r = tau_r >> lvl
                mask = jnp.logical_and((grp_r & 1) == 1, (tau_c >> lvl) == grp_r - 1)
                a_mat = a_mat + jnp.where(mask, a_lvl, 0.0)
                if lvl < 5:
                    nxt = pltpu.roll(b_m, m if rev else (CHUNK - m), axis=0)
                    b_m = jnp.where(((tau_row >> lvl) & 1) == 1, b_m, nxt)

            st = st_scr[...]
            o_c = (jnp.dot(a_mat.astype(BF16), vc.astype(BF16), preferred_element_type=F32)
                   + _dot_nt((qs * jnp.exp(cum)).astype(BF16), st.astype(BF16)))
            kd = kc * jnp.exp(total - cum)
            st_scr[...] = st * jnp.exp(total) + _dot_tn(vc.astype(BF16), kd.astype(BF16))

            if not rev:
                ofwd_scr[rows, :] = o_c
            else:
                tot = ofwd_scr[rows, :] + o_c
                y = tot * lax.rsqrt(jnp.mean(tot * tot, axis=-1, keepdims=True) + EPS) * nrm
                o_ref[0, rows, :] = (y * _silu(g_ref[0, rows, :])).astype(o_ref.dtype)
            return carry

        lax.fori_loop(0, n_chunks, chunk_body, 0)
        if want_state:
            sfin_ref[0, d, 0] = st_scr[...].T


def gla_pallas(proj, lb, hg_norm, s0, want_state):
    bsz, seq_len, _ = proj.shape
    n_chunks = seq_len // CHUNK
    has_s0 = s0 is not None

    def col(group):
        return pl.BlockSpec((1, seq_len, LANES), lambda b, h, group=group: (b, 0, group * HG_HEADS + h))

    in_specs = [col(0), col(1), col(2), col(3), col(4),
                pl.BlockSpec((2, LANES), lambda b, h: (0, h)),
                pl.BlockSpec((1, LANES), lambda b, h: (0, 0))]
    args = [proj, proj, proj, proj, proj, lb, hg_norm.reshape(1, LANES)]
    if has_s0:
        in_specs.append(pl.BlockSpec((1, 2, 1, HG_DK, HG_DV), lambda b, h: (b, 0, h, 0, 0)))
        args.append(s0)
    out_specs = [pl.BlockSpec((1, seq_len, LANES), lambda b, h: (b, 0, h))]
    out_shape = [jax.ShapeDtypeStruct((bsz, seq_len, MIX_W), BF16)]
    if want_state:
        out_specs.append(pl.BlockSpec((1, 2, 1, HG_DK, HG_DV), lambda b, h: (b, 0, h, 0, 0)))
        out_shape.append(jax.ShapeDtypeStruct((bsz, 2, HG_HEADS, HG_DK, HG_DV), F32))
    res = pl.pallas_call(
        functools.partial(_gla_kernel, n_chunks=n_chunks, has_s0=has_s0, want_state=want_state),
        grid=(bsz, HG_HEADS),
        in_specs=in_specs, out_specs=out_specs, out_shape=out_shape,
        scratch_shapes=[pltpu.VMEM((seq_len, LANES), F32), pltpu.VMEM((HG_DV, HG_DK), F32)],
        compiler_params=_cparams("arbitrary", "arbitrary"),
        name="hgrn2_scan",
    )(*args)
    return (res[0], res[1]) if want_state else (res[0], None)


def _swap_halves64(x):
    lane = lax.broadcasted_iota(jnp.int32, x.shape, 1)
    return jnp.where((lane & 63) < 32, pltpu.roll(x, 96, axis=1), pltpu.roll(x, 32, axis=1))


def _mla_kernel(*refs, seq_len, past, latent):
    cq_ref, ckv_ref, kpe_ref, g_ref, qn_ref, kvn_ref, wuq_ref, wukv_ref = refs[:8]
    pos = 8
    if latent:
        ctx_ckv_ref, ctx_kpe_ref, cosk_ref, sink_ref, cosq_ref, sinq_ref = refs[pos:pos + 6]
        pos += 6
    o_ref = refs[pos]
    pos += 1
    if not latent:
        ckv_out_ref, kpe_out_ref = refs[pos:pos + 2]
        pos += 2
    kn_scr, v_scr, kpe2_scr = refs[pos:pos + 3]

    @pl.when(pl.program_id(1) == 0)
    def _build_keys():
        wukv = wukv_ref[...]

        def put(rows_bf16, base, n):
            kv = jnp.dot(rows_bf16, wukv, preferred_element_type=F32)
            for h in range(MLA_HEADS):
                kn_scr[h, base:base + n, :] = kv[:, h * 256:h * 256 + 128].astype(BF16)
                v_scr[h, base:base + n, :] = kv[:, h * 256 + 128:(h + 1) * 256].astype(BF16)

        if latent:
            put(ctx_ckv_ref[0, 0].astype(BF16), 0, past)
            kpe2_scr[0:past, :] = ctx_kpe_ref[0, 0].astype(BF16)
        kvn = kvn_ref[...]
        for r0 in range(0, seq_len, KV_ROWS):
            n = min(KV_ROWS, seq_len - r0)
            c = ckv_ref[0, r0:r0 + n, :]
            cn = c * lax.rsqrt(jnp.mean(c * c, axis=-1, keepdims=True) + EPS) * kvn
            if not latent:
                ckv_out_ref[0, r0:r0 + n, :] = cn
            put(cn.astype(BF16), past + r0, n)
        kp = kpe_ref[0]
        if latent:
            kp = kp * cosk_ref[...] + _swap_halves64(kp) * sink_ref[...]
        else:
            kpe_out_ref[0] = kp[:, :MLA_ROPE]
        kpe2_scr[past:past + seq_len, :] = (kp + pltpu.roll(kp, MLA_ROPE, axis=1)).astype(BF16)

    cq = cq_ref[0]
    cqn = cq * lax.rsqrt(jnp.mean(cq * cq, axis=-1, keepdims=True) + EPS) * qn_ref[...]
    q = jnp.dot(cqn.astype(BF16), wuq_ref[...], preferred_element_type=F32)
    lane = lax.broadcasted_iota(jnp.int32, (q.shape[0], LANES), 1)
    kpe2 = kpe2_scr[...]
    for h in range(MLA_HEADS):
        qn = q[:, h * 128:(h + 1) * 128]
        p0 = MLA_HEADS * MLA_NOPE + (h // 2) * LANES
        qp = q[:, p0:p0 + LANES]
        if latent:
            qp = qp * cosq_ref[...] + _swap_halves64(qp) * sinq_ref[...]
        qp = jnp.where((lane >> 6) == (h % 2), qp, 0.0)
        s = (_dot_nt(qn.astype(BF16), kn_scr[h]) + _dot_nt(qp.astype(BF16), kpe2)) * MLA_SCALE
        e = jnp.exp(s - jnp.max(s, axis=-1, keepdims=True))
        p = e / jnp.sum(e, axis=-1, keepdims=True)
        o_h = jnp.dot(p.astype(BF16), v_scr[h], preferred_element_type=F32)
        o_ref[0, :, h * 128:(h + 1) * 128] = (o_h * _silu(g_ref[0, :, h * 128:(h + 1) * 128])).astype(o_ref.dtype)


def mla_pallas(proj, q_norm, kv_norm, w_uq_p, w_ukv, ctx_ckv=None, ctx_kpe2=None, e_idx=0, rope=None):
    bsz, seq_len, _ = proj.shape
    latent = ctx_ckv is not None
    past = ctx_ckv.shape[2] if latent else 0
    lk = past + seq_len
    tq = min(MLA_TQ, seq_len)
    nq = seq_len // tq
    in_specs = [
        pl.BlockSpec((1, tq, 512), lambda b, i: (b, i, 10)),
        pl.BlockSpec((1, seq_len, 512), lambda b, i: (b, 0, 11)),
        pl.BlockSpec((1, seq_len, LANES), lambda b, i: (b, 0, 56)),
        pl.BlockSpec((1, tq, 1024), lambda b, i: (b, i, 6)),
        pl.BlockSpec((1, 512), lambda b, i: (0, 0)),
        pl.BlockSpec((1, 512), lambda b, i: (0, 0)),
        pl.BlockSpec((512, 1536), lambda b, i: (0, 0)),
        pl.BlockSpec((512, 2048), lambda b, i: (0, 0)),
    ]
    args = [proj, proj, proj, proj, q_norm.reshape(1, 512), kv_norm.reshape(1, 512), w_uq_p, w_ukv]
    if latent:
        cos_t, sin_t = rope
        in_specs += [
            pl.BlockSpec((1, 1, past, 512), lambda b, i: (b, e_idx, 0, 0)),
            pl.BlockSpec((1, 1, past, LANES), lambda b, i: (b, e_idx, 0, 0)),
            pl.BlockSpec((seq_len, LANES), lambda b, i: (0, 0)),
            pl.BlockSpec((seq_len, LANES), lambda b, i: (0, 0)),
            pl.BlockSpec((tq, LANES), lambda b, i: (i, 0)),
            pl.BlockSpec((tq, LANES), lambda b, i: (i, 0)),
        ]
        args += [ctx_ckv, ctx_kpe2, cos_t, sin_t, cos_t, sin_t]
    out_specs = [pl.BlockSpec((1, tq, 1024), lambda b, i: (b, i, 0))]
    out_shape = [jax.ShapeDtypeStruct((bsz, seq_len, MIX_W), BF16)]
    if not latent:
        out_specs += [pl.BlockSpec((1, seq_len, 512), lambda b, i: (b, 0, 0)),
                      pl.BlockSpec((1, seq_len, MLA_ROPE), lambda b, i: (b, 0, 0))]
        out_shape += [jax.ShapeDtypeStruct((bsz, seq_len, MLA_KV_RANK), F32),
                      jax.ShapeDtypeStruct((bsz, seq_len, MLA_ROPE), F32)]
    return pl.pallas_call(
        functools.partial(_mla_kernel, seq_len=seq_len, past=past, latent=latent),
        grid=(bsz, nq),
        in_specs=in_specs, out_specs=out_specs, out_shape=out_shape,
        scratch_shapes=[pltpu.VMEM((MLA_HEADS, lk, LANES), BF16), pltpu.VMEM((MLA_HEADS, lk, LANES), BF16),
                        pltpu.VMEM((lk, LANES), BF16)],
        compiler_params=_cparams("arbitrary", "arbitrary"),
        name="mla_attention",
    )(*args)


def axial_rope(n_tokens):
    rows = n_tokens // GRID_W
    row = jnp.repeat(jnp.arange(rows, dtype=jnp.float32), GRID_W)
    col = jnp.arange(n_tokens, dtype=jnp.int32) % GRID_W
    half = MLA_ROPE // 2
    inv_freq = ROPE_BASE ** (-jnp.arange(0, half, 2, dtype=jnp.float32) / half)
    ang = jnp.concatenate([row[:, None] * inv_freq, col.astype(jnp.float32)[:, None] * inv_freq], axis=-1)
    return jnp.cos(ang), jnp.sin(ang)


def rope_tables(cos, sin):
    return jnp.tile(cos, (1, 4)), jnp.tile(jnp.concatenate([-sin, sin], axis=-1), (1, 2))


def permute_even_in(w):
    return jnp.concatenate([w[:, :6144], w[:, 6208:7232], w[:, 6144:6208],
                            jnp.zeros((w.shape[0], EVEN_IN_PAD - EVEN_IN), w.dtype)], axis=1)


def permute_w_uq(w):
    w3 = w.reshape(w.shape[0], MLA_HEADS, MLA_NOPE + MLA_ROPE)
    return jnp.concatenate([w3[:, :, :MLA_NOPE].reshape(w.shape[0], -1),
                            w3[:, :, MLA_NOPE:].reshape(w.shape[0], -1)], axis=1)


def _mask_matmul_exact(mask_bf16, x):
    x_hi = x.astype(BF16)
    r1 = x - x_hi.astype(F32)
    x_mid = r1.astype(BF16)
    x_lo = (r1 - x_mid.astype(F32)).astype(BF16)
    return (jnp.dot(mask_bf16, x_hi, preferred_element_type=F32) + jnp.dot(mask_bf16, x_mid, preferred_element_type=F32)
            + jnp.dot(mask_bf16, x_lo, preferred_element_type=F32))


GDN_PRE_GROUP = 2


def _gdn_kernel(*refs, seq_len, has_s0, want_state):
    q_ref, k_ref, v_ref, z_ref, ab_ref, wq_ref, wk_ref, wv_ref, alog_ref, dt_ref, nrm_ref = refs[:11]
    pos = 11
    s0_ref = None
    if has_s0:
        s0_ref = refs[pos]
        pos += 1
    o_ref = refs[pos]
    pos += 1
    sfin_ref = None
    if want_state:
        sfin_ref = refs[pos]
        pos += 1
    qn_scr, kn_scr, vn_scr, gb_scr = refs[pos:pos + 4]
    pos += 4
    per_dir = [refs[pos + 8 * d:pos + 8 * d + 8] for d in (0, 1)]
    n_chunks = seq_len // CHUNK
    head = pl.program_id(1)

    row = lax.broadcasted_iota(jnp.int32, (seq_len, LANES), 0)

    def conv_silu(x_ref, w_ref):
        x = x_ref[0]
        w = w_ref[...]
        x_prev = jnp.where(row == 0, 0.0, pltpu.roll(x, 1, axis=0))
        x_next = jnp.where(row == seq_len - 1, 0.0, pltpu.roll(x, seq_len - 1, axis=0))
        return _silu(x_prev * w[0:1, :] + x * w[1:2, :] + x_next * w[2:3, :])

    def l2n(x):
        return x * lax.rsqrt(jnp.sum(x * x, axis=-1, keepdims=True) + EPS)

    qn_scr[...] = l2n(conv_silu(q_ref, wq_ref)) * GDN_DK ** -0.5
    kn_scr[...] = l2n(conv_silu(k_ref, wk_ref))
    vn_scr[...] = conv_silu(v_ref, wv_ref)
    ab = ab_ref[0]
    lane_full = lax.broadcasted_iota(jnp.int32, (seq_len, LANES), 1)
    pre = ab + dt_ref[...]
    softplus = jnp.maximum(pre, 0.0) + jnp.log1p(jnp.exp(-jnp.abs(pre)))
    gb_scr[...] = jnp.where(lane_full < 2 * GDN_HEADS, -jnp.exp(alog_ref[...]) * softplus, jax.nn.sigmoid(ab))

    r64 = lax.broadcasted_iota(jnp.int32, (CHUNK, CHUNK), 0)
    c64 = lax.broadcasted_iota(jnp.int32, (CHUNK, CHUNK), 1)
    lane = lax.broadcasted_iota(jnp.int32, (CHUNK, LANES), 1)
    eye = r64 == c64
    eye_f = eye.astype(F32)

    def load_chunk(cidx):
        rows = pl.ds(pl.multiple_of(cidx * CHUNK, CHUNK), CHUNK)
        return qn_scr[rows, :], kn_scr[rows, :], vn_scr[rows, :], gb_scr[rows, :]

    def prepare(vals, d):
        rev = d == 1
        qc, kc, vc, gb = vals
        tau_r = (CHUNK - 1 - r64) if rev else r64
        tau_c = (CHUNK - 1 - c64) if rev else c64
        incl = tau_c <= tau_r
        strict = tau_c < tau_r
        g_lane = d * GDN_HEADS + head
        b_lane = 2 * GDN_HEADS + d * GDN_HEADS + head
        cum_all = _mask_matmul_exact(incl.astype(BF16), gb)
        cum_col = jnp.sum(jnp.where(lane == g_lane, cum_all, 0.0), axis=-1, keepdims=True)
        beta_col = jnp.sum(jnp.where(lane == b_lane, gb, 0.0), axis=-1, keepdims=True)
        cum_row = jnp.sum(jnp.where(eye, cum_col, 0.0), axis=0, keepdims=True)
        total = cum_col[0:1, :] if rev else cum_col[CHUNK - 1:CHUNK, :]
        decay = jnp.exp(jnp.minimum(cum_col - cum_row, 0.0))
        k_beta = kc * beta_col
        kc_bf = kc.astype(BF16)
        kk = _dot_nt(k_beta.astype(BF16), kc_bf)
        lower = jnp.where(strict, kk * decay, 0.0)
        t_mat = eye_f - jnp.where(jnp.logical_and((tau_r & 1) == 1, tau_c == tau_r - 1), lower, 0.0)
        for lvl in range(1, 6):
            grp_r = tau_r >> lvl
            mask = jnp.logical_and((grp_r & 1) == 1, (tau_c >> lvl) == grp_r - 1)
            c_m = jnp.where(mask, lower, 0.0)
            t_mat = t_mat - _dot3(_dot3(t_mat, c_m), t_mat)
        t_bf = t_mat.astype(BF16)
        e_cum = jnp.exp(cum_col)
        w = jnp.dot(t_bf, (k_beta * e_cum).astype(BF16), preferred_element_type=F32).astype(BF16)
        u = jnp.dot(t_bf, (vc * beta_col).astype(BF16), preferred_element_type=F32)
        attn = jnp.where(incl, _dot_nt(qc.astype(BF16), kc_bf) * decay, 0.0).astype(BF16)
        qe = (qc * e_cum).astype(BF16)
        kd = (kc * jnp.exp(total - cum_col)).astype(BF16)
        return w, u, qe, kd, attn, jnp.broadcast_to(jnp.exp(total), (1, LANES))

    def store_chunk(cidx, d, res):
        rows = pl.ds(pl.multiple_of(cidx * CHUNK, CHUNK), CHUNK)
        w_scr, u_scr, qe_scr, kd_scr, at_scr, et_scr = per_dir[d][:6]
        w_scr[rows, :], u_scr[rows, :], qe_scr[rows, :], kd_scr[rows, :], at_scr[rows, :] = res[:5]
        et_scr[cidx] = res[5]

    def pre_body(gi, carry):
        idx = [gi * GDN_PRE_GROUP + j for j in range(GDN_PRE_GROUP)]
        vals = [load_chunk(c) for c in idx]
        res = [[prepare(v, d) for d in (0, 1)] for v in vals]
        for c, r in zip(idx, res):
            for d in (0, 1):
                store_chunk(c, d, r[d])
        return carry

    lax.fori_loop(0, n_chunks // GDN_PRE_GROUP, pre_body, 0)

    for d in (0, 1):
        s_scr = per_dir[d][7]
        if has_s0:
            s_scr[...] = s0_ref[0, d, 0]
        else:
            s_scr[...] = jnp.zeros_like(s_scr)

    def load_step(cidx, d):
        w_scr, u_scr, qe_scr, kd_scr, at_scr, et_scr, _, s_scr = per_dir[d]
        rows = pl.ds(pl.multiple_of(cidx * CHUNK, CHUNK), CHUNK)
        return (w_scr[rows, :], u_scr[rows, :], qe_scr[rows, :], kd_scr[rows, :], at_scr[rows, :],
                et_scr[cidx], s_scr[...])

    def step(vals):
        w, u, qe, kd, attn, et, s = vals
        s_bf = s.astype(BF16)
        v_new = u - jnp.dot(w, s_bf, preferred_element_type=F32)
        v_new_bf = v_new.astype(BF16)
        o = jnp.dot(qe, s_bf, preferred_element_type=F32) + jnp.dot(attn, v_new_bf, preferred_element_type=F32)
        return o, et[:, 0:1] * s + _dot_tn(kd, v_new_bf)

    def rec_body(ci, carry):
        idx = (ci, n_chunks - 1 - ci)
        vals = [load_step(idx[d], d) for d in (0, 1)]
        res = [step(v) for v in vals]
        for d in (0, 1):
            rows = pl.ds(pl.multiple_of(idx[d] * CHUNK, CHUNK), CHUNK)
            per_dir[d][6][rows, :] = res[d][0]
            per_dir[d][7][...] = res[d][1]
        return carry

    lax.fori_loop(0, n_chunks, rec_body, 0)

    tot = per_dir[0][6][...] + per_dir[1][6][...]
    y = tot * lax.rsqrt(jnp.mean(tot * tot, axis=-1, keepdims=True) + EPS) * nrm_ref[...]
    o_ref[0] = (y * _silu(z_ref[0])).astype(o_ref.dtype)
    if want_state:
        for d in (0, 1):
            sfin_ref[0, d, 0] = per_dir[d][7][...]


def gdn_pallas(proj, conv_w, a_log, dt_bias, gdn_norm, s0, want_state):
    bsz, seq_len, _ = proj.shape
    has_s0 = s0 is not None

    def col(group):
        return pl.BlockSpec((1, seq_len, LANES), lambda b, h, group=group: (b, 0, group * GDN_HEADS + h))

    def wcol(group):
        return pl.BlockSpec((CONV_W, LANES), lambda b, h, group=group: (0, group * GDN_HEADS + h))

    vec = pl.BlockSpec((1, LANES), lambda b, h: (0, 0))
    pad = jnp.zeros((LANES - 2 * GDN_HEADS,), F32)
    alog_vec = jnp.concatenate([a_log.reshape(-1), pad]).reshape(1, LANES)
    dt_vec = jnp.concatenate([dt_bias.reshape(-1), pad]).reshape(1, LANES)
    gate_block = (3 * MIX_W + MIX_W + 3 * HY_CH + HY_CH) // LANES
    in_specs = [col(0), col(1), col(2), col(3),
                pl.BlockSpec((1, seq_len, LANES), lambda b, h: (b, 0, gate_block)),
                wcol(0), wcol(1), wcol(2), vec, vec, vec]
    args = [proj, proj, proj, proj, proj, conv_w, conv_w, conv_w, alog_vec, dt_vec, gdn_norm.reshape(1, LANES)]
    if has_s0:
        in_specs.append(pl.BlockSpec((1, 2, 1, GDN_DK, GDN_DV), lambda b, h: (b, 0, h, 0, 0)))
        args.append(s0)
    out_specs = [pl.BlockSpec((1, seq_len, LANES), lambda b, h: (b, 0, h))]
    out_shape = [jax.ShapeDtypeStruct((bsz, seq_len, MIX_W), BF16)]
    if want_state:
        out_specs.append(pl.BlockSpec((1, 2, 1, GDN_DK, GDN_DV), lambda b, h: (b, 0, h, 0, 0)))
        out_shape.append(jax.ShapeDtypeStruct((bsz, 2, GDN_HEADS, GDN_DK, GDN_DV), F32))
    seq_f32 = pltpu.VMEM((seq_len, LANES), F32)
    seq_bf16 = pltpu.VMEM((seq_len, LANES), BF16)
    one_dir = [seq_bf16, seq_f32, seq_bf16, seq_bf16, pltpu.VMEM((seq_len, CHUNK), BF16),
               pltpu.VMEM((seq_len // CHUNK, 1, LANES), F32), seq_f32, pltpu.VMEM((GDN_DK, GDN_DV), F32)]
    res = pl.pallas_call(
        functools.partial(_gdn_kernel, seq_len=seq_len, has_s0=has_s0, want_state=want_state),
        grid=(bsz, GDN_HEADS),
        in_specs=in_specs, out_specs=out_specs, out_shape=out_shape,
        scratch_shapes=[seq_f32, seq_f32, seq_f32, seq_f32] + one_dir + one_dir,
        compiler_params=_cparams("arbitrary", "arbitrary"),
        name="gdn_scan",
    )(*args)
    return (res[0], res[1]) if want_state else (res[0], None)


def permute_odd_in(w):
    return jnp.concatenate([w[:, :3072], w[:, 3104:4128], w[:, 4128:7200], w[:, 7200:8224], w[:, 3072:3104],
                            jnp.zeros((w.shape[0], ODD_IN_PAD - ODD_IN), w.dtype)], axis=1)


def _mm3(m_hi, m_lo, x_hi, x_lo):
    return (jnp.dot(m_hi, x_hi, preferred_element_type=F32) + jnp.dot(m_hi, x_lo, preferred_element_type=F32)
            + jnp.dot(m_lo, x_hi, preferred_element_type=F32))


def _mm3_tn(m_hi, m_lo, x_hi, x_lo):
    return _dot_tn(m_hi, x_hi) + _dot_tn(m_hi, x_lo) + _dot_tn(m_lo, x_hi)


def dft_mats(seq_len):
    n = 2 * seq_len
    k = jnp.arange(seq_len, dtype=jnp.int32)[:, None]
    t = jnp.arange(seq_len, dtype=jnp.int32)[None, :]
    ang = ((k * t) & (n - 1)).astype(F32) * (2.0 * math.pi / n)
    return _split_bf16(jnp.cos(ang)) + _split_bf16(jnp.sin(ang))


def filter_features(n):
    t = jnp.linspace(0.0, 1.0, n, dtype=F32)[:, None]
    w = 2.0 * math.pi * jnp.arange(n, dtype=F32)[:, None] / n
    f = jnp.linspace(1e-4, HY_BANDS - 1, HY_BANDS, dtype=F32)[None, :]
    z = jnp.concatenate([t, jnp.cos(f * w), -jnp.sin(f * w)], axis=-1)
    return jnp.pad(z, ((0, 0), (0, LANES - HY_EMB))), t


def filter_deltas():
    max_decay = math.log(HY_TARGET) / HY_DECAY_PCT_SHORT
    min_decay = math.log(HY_TARGET) / HY_DECAY_PCT_LONG
    return jnp.abs(jnp.linspace(min_decay, max_decay, HY_CH, dtype=F32))[None, :]


def _filter_kernel(z_ref, t_ref, w1_ref, b1_ref, w2_ref, b2_ref, w3f_ref, w3b_ref, dl_ref,
                   chi_ref, clo_ref, shi_ref, slo_ref, hr_ref, hi_ref, hn_ref,
                   sum_hi, sum_lo, dif_hi, dif_lo):
    @pl.when(pl.program_id(1) == 0)
    def _make_filter():
        hid = jnp.sin(jnp.dot(z_ref[...].astype(BF16), w1_ref[...].astype(BF16), preferred_element_type=F32)
                      + b1_ref[...])
        hid = jnp.sin(jnp.dot(hid.astype(BF16), w2_ref[...].astype(BF16), preferred_element_type=F32)
                      + b2_ref[...])
        hid = hid.astype(BF16)
        win = jnp.exp(-t_ref[...] * dl_ref[...]) + HY_SHIFT
        hf = jnp.dot(hid, w3f_ref[...].astype(BF16), preferred_element_type=F32) * win
        hb = jnp.dot(hid, w3b_ref[...].astype(BF16), preferred_element_type=F32) * win
        row = lax.broadcasted_iota(jnp.int32, hb.shape, 0)
        hb = jnp.where(row == 0, 0.0, hb)
        inv = 1.0 / jnp.sum(jnp.abs(hf) + jnp.abs(hb), axis=0, keepdims=True)
        hsum = (hf + hb) * inv
        hdif = (hb - hf) * inv
        sign = (1 - 2 * (row & 1)).astype(F32)
        hn_ref[...] = jnp.sum(hsum * sign, axis=0, keepdims=True)
        sum_hi[...], sum_lo[...] = _split_bf16(hsum)
        dif_hi[...], dif_lo[...] = _split_bf16(hdif)

    hr_ref[...] = _mm3(chi_ref[...], clo_ref[...], sum_hi[...], sum_lo[...])
    hi_ref[...] = _mm3(shi_ref[...], slo_ref[...], dif_hi[...], dif_lo[...])


def hyena_filter_spectrum(seq_len, w1, b1, w2, b2, w3, mats):
    z, t = filter_features(seq_len)
    w1p = jnp.pad(w1, ((0, LANES - HY_EMB), (0, 0)))
    kb = min(HY_KB, seq_len)
    nj = HY_CH // HY_TC

    def full(shape):
        return pl.BlockSpec(shape, lambda j, k: (0,) * len(shape))

    mat = pl.BlockSpec((kb, seq_len), lambda j, k: (k, 0))
    return pl.pallas_call(
        _filter_kernel,
        grid=(nj, seq_len // kb),
        in_specs=[full((seq_len, LANES)), full((seq_len, 1)), full((LANES, HY_HIDDEN)), full((1, HY_HIDDEN)),
                  full((HY_HIDDEN, HY_HIDDEN)), full((1, HY_HIDDEN)),
                  pl.BlockSpec((HY_HIDDEN, HY_TC), lambda j, k: (0, j)),
                  pl.BlockSpec((HY_HIDDEN, HY_TC), lambda j, k: (0, nj + j)),
                  pl.BlockSpec((1, HY_TC), lambda j, k: (0, j)),
                  mat, mat, mat, mat],
        out_specs=[pl.BlockSpec((kb, HY_TC), lambda j, k: (k, j)),
                   pl.BlockSpec((kb, HY_TC), lambda j, k: (k, j)),
                   pl.BlockSpec((1, HY_TC), lambda j, k: (0, j))],
        out_shape=[jax.ShapeDtypeStruct((seq_len, HY_CH), F32), jax.ShapeDtypeStruct((seq_len, HY_CH), F32),
                   jax.ShapeDtypeStruct((1, HY_CH), F32)],
        scratch_shapes=[pltpu.VMEM((seq_len, HY_TC), BF16)] * 4,
        compiler_params=_cparams("arbitrary", "arbitrary"),
        name="hyena_filter",
    )(z, t, w1p, b1.reshape(1, -1), w2, b2.reshape(1, -1), w3, w3, filter_deltas(), *mats)


def _hyena_kernel(v_ref, x0_ref, x1_ref, g_ref, wv_ref, w0_ref, w1_ref, bv_ref, b0_ref, b1_ref, skip_ref,
                  hr_ref, hi_ref, hn_ref, chi_ref, clo_ref, shi_ref, slo_ref, o_ref,
                  u_scr, uhi_scr, ulo_scr, x0g_scr, y_scr, unyq_scr, *, seq_len, kb):
    kk = pl.program_id(2)

    @pl.when(kk == 0)
    def _prologue():
        row = lax.broadcasted_iota(jnp.int32, (seq_len, HY_TC), 0)

        def conv(x_ref, w_ref, b_ref):
            x = x_ref[0]
            w = w_ref[...]
            x_prev = jnp.where(row == 0, 0.0, pltpu.roll(x, 1, axis=0))
            x_next = jnp.where(row == seq_len - 1, 0.0, pltpu.roll(x, seq_len - 1, axis=0))
            return x_prev * w[0:1, :] + x * w[1:2, :] + x_next * w[2:3, :] + b_ref[...]

        u = conv(v_ref, wv_ref, bv_ref) * conv(x1_ref, w1_ref, b1_ref)
        u_scr[...] = u
        uhi_scr[...], ulo_scr[...] = _split_bf16(u)
        x0g_scr[...] = conv(x0_ref, w0_ref, b0_ref) * _silu(g_ref[0])
        sign = (1 - 2 * (row & 1)).astype(F32)
        unyq_scr[...] = jnp.sum(u * sign, axis=0, keepdims=True)
        y_scr[...] = jnp.zeros_like(y_scr)

    chi, clo, shi, slo = chi_ref[...], clo_ref[...], shi_ref[...], slo_ref[...]
    a = _mm3(chi, clo, uhi_scr[...], ulo_scr[...])
    b = _mm3(shi, slo, uhi_scr[...], ulo_scr[...])
    hr, hi = hr_ref[...], hi_ref[...]
    grow = kk * kb + lax.broadcasted_iota(jnp.int32, a.shape, 0)
    pr = (a * hr + b * hi) * jnp.where(grow == 0, 1.0, 2.0)
    pi = (b * hr - a * hi) * 2.0
    pr_hi, pr_lo = _split_bf16(pr)
    pi_hi, pi_lo = _split_bf16(pi)
    y_scr[...] += _mm3_tn(chi, clo, pr_hi, pr_lo) + _mm3_tn(shi, slo, pi_hi, pi_lo)

    @pl.when(kk == pl.num_programs(2) - 1)
    def _epilogue():
        row = lax.broadcasted_iota(jnp.int32, (seq_len, HY_TC), 0)
        sign = (1 - 2 * (row & 1)).astype(F32)
        y = (y_scr[...] + sign * (unyq_scr[...] * hn_ref[...])) * (1.0 / (2 * seq_len))
        o_ref[0] = (x0g_scr[...] * (y + u_scr[...] * skip_ref[...])).astype(o_ref.dtype)


def hyena_pallas(proj, conv_w, conv_b, skip, spectrum, mats):
    bsz, seq_len, _ = proj.shape
    hr, hi, hn = spectrum
    kb = min(HY_KB, seq_len)
    nj = HY_CH // HY_TC
    base = 4 * MIX_W // HY_TC

    def col(group):
        return pl.BlockSpec((1, seq_len, HY_TC), lambda j, b, k, group=group: (b, 0, base + group * nj + j))

    def wcol(group):
        return pl.BlockSpec((CONV_W, HY_TC), lambda j, b, k, group=group: (0, group * nj + j))

    def bcol(group):
        return pl.BlockSpec((1, HY_TC), lambda j, b, k, group=group: (0, group * nj + j))

    mat = pl.BlockSpec((kb, seq_len), lambda j, b, k: (k, 0))
    spec = pl.BlockSpec((kb, HY_TC), lambda j, b, k: (k, j))
    vec = pl.BlockSpec((1, HY_TC), lambda j, b, k: (0, j))
    conv_b2 = conv_b.reshape(1, -1)
    return pl.pallas_call(
        functools.partial(_hyena_kernel, seq_len=seq_len, kb=kb),
        grid=(nj, bsz, seq_len // kb),
        in_specs=[col(0), col(1), col(2), col(3), wcol(0), wcol(1), wcol(2), bcol(0), bcol(1), bcol(2), vec,
                  spec, spec, vec, mat, mat, mat, mat],
        out_specs=pl.BlockSpec((1, seq_len, HY_TC), lambda j, b, k: (b, 0, j)),
        out_shape=jax.ShapeDtypeStruct((bsz, seq_len, HY_CH), BF16),
        scratch_shapes=[pltpu.VMEM((seq_len, HY_TC), F32), pltpu.VMEM((seq_len, HY_TC), BF16),
                        pltpu.VMEM((seq_len, HY_TC), BF16), pltpu.VMEM((seq_len, HY_TC), F32),
                        pltpu.VMEM((seq_len, HY_TC), F32), pltpu.VMEM((1, HY_TC), F32)],
        compiler_params=_cparams("arbitrary", "arbitrary", "arbitrary"),
        name="hyena_conv",
    )(proj, proj, proj, proj, conv_w, conv_w, conv_w, conv_b2, conv_b2, conv_b2, skip.reshape(1, -1),
      hr, hi, hn, *mats)


def kernel(x_prompt, x_sample, state_hgrn, cache_mla_ckv, cache_mla_kpe, state_gdn, c, c_ctx,
           ada_w, ada_b, norm_pre, norm_post, ev_in_w, ev_out_w, hg_lb_logits, hg_norm,
           mla_q_norm, mla_w_uq, mla_kv_norm, mla_w_ukv, od_in_w, od_out_w, gdn_conv, gdn_a_log,
           gdn_dt_bias, gdn_norm, hy_conv_w, hy_conv_b, hy_w1, hy_b1, hy_w2, hy_b2, hy_w3, hy_skip):
    bp, lp, d = x_prompt.shape
    bs, ls, _ = x_sample.shape

    lb_cum = jnp.cumsum(jax.nn.softmax(hg_lb_logits.astype(jnp.float32), axis=1), axis=1)
    hg_lb = lb_cum - lb_cum[:, :1]

    cond = jnp.concatenate([c_ctx[None], c, jnp.zeros((8 - 1 - bs, d), c.dtype)], axis=0)
    mods = adaln_all(cond, ada_w, ada_b)

    rope = rope_tables(*axial_rope(ls))
    ctx_kpe2 = jnp.concatenate([cache_mla_kpe, cache_mla_kpe], axis=-1)
    mats_p, mats_s = dft_mats(lp), dft_mats(ls)

    xp = x_prompt.reshape(bp * lp, d)
    xs = x_sample.reshape(bs * ls, d)
    new_hg, new_ckv, new_kpe, new_gdn = [], [], [], []
    for l in range(DEPTH):
        shift, scale, gate = jnp.split(mods[l], 3, axis=-1)
        shift_p, scale_p, gate_p = (t[0:1, None, :] for t in (shift, scale, gate))
        shift_s, scale_s, gate_s = (t[1:1 + bs, None, :] for t in (shift, scale, gate))
        gain_pre = norm_pre[l][None]
        gain_post = norm_post[l][None]
        if l % 2 == 0:
            e = l // 2
            w_in = permute_even_in(ev_in_w[e]).astype(BF16)
            w_out = ev_out_w[e].astype(BF16)
            tn = EVEN_TN
        else:
            o = l // 2
            w_in = permute_odd_in(od_in_w[o]).astype(BF16)
            w_out = od_out_w[o].astype(BF16)
            tn = ODD_TN
        proj_p = in_proj(xp, gain_pre, scale_p, shift_p, w_in, tn, bp * lp).reshape(bp, lp, -1)
        proj_s = in_proj(xs, gain_pre, scale_s, shift_s, w_in, tn, ls).reshape(bs, ls, -1)
        if l % 2 == 0:
            lb = hg_lb[:, l]
            w_uq = permute_w_uq(mla_w_uq[e]).astype(BF16)
            w_ukv = mla_w_ukv[e].astype(BF16)
            oa_p, s_hg = gla_pallas(proj_p, lb, hg_norm[e], None, True)
            oa_s, _ = gla_pallas(proj_s, lb, hg_norm[e], state_hgrn[:, e], False)
            ob_p, ckv, kpe = mla_pallas(proj_p, mla_q_norm[e], mla_kv_norm[e], w_uq, w_ukv)
            ob_s, = mla_pallas(proj_s, mla_q_norm[e], mla_kv_norm[e], w_uq, w_ukv,
                               cache_mla_ckv, ctx_kpe2, e, rope)
            new_hg.append(s_hg)
            new_ckv.append(ckv)
            new_kpe.append(kpe)
        else:
            oa_p, s_gdn = gdn_pallas(proj_p, gdn_conv[o], gdn_a_log[o], gdn_dt_bias[o], gdn_norm[o], None, True)
            oa_s, _ = gdn_pallas(proj_s, gdn_conv[o], gdn_a_log[o], gdn_dt_bias[o], gdn_norm[o],
                                 state_gdn[:, o], False)
            fw = (hy_w1[o], hy_b1[o], hy_w2[o], hy_b2[o], hy_w3[o])
            ob_p = hyena_pallas(proj_p, hy_conv_w[o], hy_conv_b[o], hy_skip[o],
                                hyena_filter_spectrum(lp, *fw, mats_p), mats_p)
            ob_s = hyena_pallas(proj_s, hy_conv_w[o], hy_conv_b[o], hy_skip[o],
                                hyena_filter_spectrum(ls, *fw, mats_s), mats_s)
            new_gdn.append(s_gdn)
        xp = out_proj(oa_p.reshape(bp * lp, -1), ob_p.reshape(bp * lp, -1), w_out, gain_post, gate_p, xp, bp * lp)
        xs = out_proj(oa_s.reshape(bs * ls, -1), ob_s.reshape(bs * ls, -1), w_out, gain_post, gate_s, xs, ls)

    return (xp.reshape(bp, lp, d), xs.reshape(bs, ls, d), jnp.stack(new_hg, axis=1),
            jnp.stack(new_ckv, axis=1), jnp.stack(new_kpe, axis=1), jnp.stack(new_gdn, axis=1))
```

```python
import functools
import math

import jax
import jax.numpy as jnp
import numpy as np
from jax import lax
from jax.experimental import pallas as pl
from jax.experimental.pallas import tpu as pltpu

D_MODEL = 2048
DEPTH = 4
GRID_W = 64
N_EVEN = (DEPTH + 1) // 2
N_ODD = DEPTH // 2
MIX_W = D_MODEL // 2
HG_HEADS = 8
HG_DK = MIX_W // HG_HEADS
HG_DV = MIX_W // HG_HEADS
MLA_HEADS = 8
MLA_NOPE = 128
MLA_ROPE = 64
MLA_V = MIX_W // MLA_HEADS
MLA_Q_RANK = D_MODEL // 4
MLA_KV_RANK = D_MODEL // 4
MLA_SCALE = (MLA_NOPE + MLA_ROPE) ** -0.5
GDN_HEADS = 8
GDN_DK = MIX_W // GDN_HEADS
GDN_DV = MIX_W // GDN_HEADS
HY_CH = MIX_W
HY_EMB = 33
HY_BANDS = (HY_EMB - 1) // 2
HY_HIDDEN = 64
HY_DECAY_PCT_SHORT = 0.3
HY_DECAY_PCT_LONG = 1.5
HY_TARGET = 1e-2
HY_SHIFT = 0.05
CONV_W = 3
CHUNK = 64
ROPE_BASE = 10000.0
EPS = 1e-6

EVEN_IN = 5 * MIX_W + MLA_Q_RANK + MLA_KV_RANK + MLA_ROPE + MIX_W
ODD_IN = 3 * MIX_W + 4 * GDN_HEADS + MIX_W + 3 * HY_CH + HY_CH

LANES = 128
VMEM_LIMIT_BYTES = 56 * 1024 * 1024

TOKEN_TILE = 256
EVEN_IN_PAD = 3 * 19 * LANES
EVEN_TN = 19 * LANES
ODD_IN_PAD = 5 * 13 * LANES
ODD_TN = 13 * LANES
MLA_TQ = 256
KV_ROWS = 512
HY_TC = 256
HY_KB = 256

F32 = jnp.float32
BF16 = jnp.bfloat16
HIGHEST = lax.Precision.HIGHEST


def _cparams(*sem):
    return pltpu.CompilerParams(dimension_semantics=sem, vmem_limit_bytes=VMEM_LIMIT_BYTES)


def _dot_nt(a, b):
    return lax.dot_general(a, b, (((1,), (1,)), ((), ())), preferred_element_type=F32)


def _dot_tn(a, b):
    return lax.dot_general(a, b, (((0,), (0,)), ((), ())), preferred_element_type=F32)


def _silu(x):
    return x * jax.nn.sigmoid(x)


def _split_bf16(a):
    hi = a.astype(BF16)
    return hi, (a - hi.astype(F32)).astype(BF16)


def _dot3(a, b):
    a_hi, a_lo = _split_bf16(a)
    b_hi, b_lo = _split_bf16(b)
    return (jnp.dot(a_hi, b_hi, preferred_element_type=F32) + jnp.dot(a_hi, b_lo, preferred_element_type=F32)
            + jnp.dot(a_lo, b_hi, preferred_element_type=F32))


def _adaln_kernel(cond_ref, w_ref, b_ref, o_ref):
    act = _silu(cond_ref[...]).astype(BF16)
    o_ref[0] = jnp.dot(act, w_ref[0].astype(BF16), preferred_element_type=F32) + b_ref[0]


def adaln_all(cond, ada_w, ada_b):
    rows = cond.shape[0]
    tn = 1024
    n3 = ada_w.shape[-1]
    return pl.pallas_call(
        _adaln_kernel,
        grid=(ada_w.shape[0], n3 // tn),
        in_specs=[
            pl.BlockSpec((rows, D_MODEL), lambda l, j: (0, 0)),
            pl.BlockSpec((1, D_MODEL, tn), lambda l, j: (l, 0, j)),
            pl.BlockSpec((1, 1, tn), lambda l, j: (l, 0, j)),
        ],
        out_specs=pl.BlockSpec((1, rows, tn), lambda l, j: (l, 0, j)),
        out_shape=jax.ShapeDtypeStruct((ada_w.shape[0], rows, n3), F32),
        compiler_params=_cparams("arbitrary", "arbitrary"),
        name="adaln_mod",
    )(cond, ada_w, ada_b.reshape(ada_b.shape[0], 1, n3))


def _in_proj_kernel(x_ref, gain_ref, scale_ref, shift_ref, w_ref, o_ref):
    x = x_ref[...]
    y = x * lax.rsqrt(jnp.mean(x * x, axis=-1, keepdims=True) + EPS)
    y = y * gain_ref[...]
    h = y * (1.0 + scale_ref[0]) + shift_ref[0]
    o_ref[...] = jnp.dot(h.astype(BF16), w_ref[...], preferred_element_type=F32)


def in_proj(x2d, gain, scale, shift, w_bf16, tn, tokens_per_mod_row):
    t, d = x2d.shape
    n = w_bf16.shape[1]
    tiles_per_row = tokens_per_mod_row // TOKEN_TILE
    return pl.pallas_call(
        _in_proj_kernel,
        grid=(n // tn, t // TOKEN_TILE),
        in_specs=[
            pl.BlockSpec((TOKEN_TILE, d), lambda j, i: (i, 0)),
            pl.BlockSpec((1, d), lambda j, i: (0, 0)),
            pl.BlockSpec((1, 1, d), lambda j, i: (i // tiles_per_row, 0, 0)),
            pl.BlockSpec((1, 1, d), lambda j, i: (i // tiles_per_row, 0, 0)),
            pl.BlockSpec((d, tn), lambda j, i: (0, j)),
        ],
        out_specs=pl.BlockSpec((TOKEN_TILE, tn), lambda j, i: (i, j)),
        out_shape=jax.ShapeDtypeStruct((t, n), F32),
        compiler_params=_cparams("arbitrary", "arbitrary"),
        name="in_proj",
    )(x2d, gain, scale, shift, w_bf16)


def _out_proj_kernel(oa_ref, ob_ref, wa_ref, wb_ref, gain_ref, gate_ref, x_ref, y_ref):
    acc = (jnp.dot(oa_ref[...], wa_ref[...], preferred_element_type=F32)
           + jnp.dot(ob_ref[...], wb_ref[...], preferred_element_type=F32))
    nrm = acc * lax.rsqrt(jnp.mean(acc * acc, axis=-1, keepdims=True) + EPS) * gain_ref[...]
    y_ref[...] = x_ref[...] + gate_ref[0] * nrm


def out_proj(oa, ob, w_bf16, gain, gate, x2d, tokens_per_mod_row):
    t, d = x2d.shape
    tiles_per_row = tokens_per_mod_row // TOKEN_TILE
    return pl.pallas_call(
        _out_proj_kernel,
        grid=(t // TOKEN_TILE,),
        in_specs=[
            pl.BlockSpec((TOKEN_TILE, MIX_W), lambda i: (i, 0)),
            pl.BlockSpec((TOKEN_TILE, MIX_W), lambda i: (i, 0)),
            pl.BlockSpec((MIX_W, d), lambda i: (0, 0)),
            pl.BlockSpec((MIX_W, d), lambda i: (1, 0)),
            pl.BlockSpec((1, d), lambda i: (0, 0)),
            pl.BlockSpec((1, 1, d), lambda i: (i // tiles_per_row, 0, 0)),
            pl.BlockSpec((TOKEN_TILE, d), lambda i: (i, 0)),
        ],
        out_specs=pl.BlockSpec((TOKEN_TILE, d), lambda i: (i, 0)),
        out_shape=jax.ShapeDtypeStruct((t, d), F32),
        compiler_params=_cparams("arbitrary"),
        name="out_proj",
    )(oa, ob, w_bf16, w_bf16, gain, gate, x2d)


def _gla_kernel(*refs, n_chunks, has_s0, want_state):
    q_ref, ffw_ref, fbw_ref, v_ref, g_ref, lb_ref, nrm_ref = refs[:7]
    pos = 7
    s0_ref = None
    if has_s0:
        s0_ref = refs[pos]
        pos += 1
    o_ref = refs[pos]
    pos += 1
    sfin_ref = None
    if want_state:
        sfin_ref = refs[pos]
        pos += 1
    ofwd_scr, st_scr = refs[pos], refs[pos + 1]

    r64 = lax.broadcasted_iota(jnp.int32, (CHUNK, CHUNK), 0)
    c64 = lax.broadcasted_iota(jnp.int32, (CHUNK, CHUNK), 1)
    row128 = lax.broadcasted_iota(jnp.int32, (CHUNK, LANES), 0)
    nrm = nrm_ref[...]

    for d in (0, 1):
        rev = d == 1
        f_ref = fbw_ref if rev else ffw_ref
        lbd = lb_ref[d:d + 1, :]
        log_lb = jnp.log(lbd)
        log_1m = jnp.log1p(-lbd)
        tau_r = (CHUNK - 1 - r64) if rev else r64
        tau_c = (CHUNK - 1 - c64) if rev else c64
        tau_row = (CHUNK - 1 - row128) if rev else row128
        tri = (tau_c <= tau_r).astype(F32)
        eye = r64 == c64
        if has_s0:
            st_scr[...] = s0_ref[0, d, 0].T
        else:
            st_scr[...] = jnp.zeros_like(st_scr)

        def chunk_body(ci, carry, rev=rev, f_ref=f_ref, log_lb=log_lb, log_1m=log_1m,
                       tau_r=tau_r, tau_c=tau_c, tau_row=tau_row, tri=tri, eye=eye):
            cidx = (n_chunks - 1 - ci) if rev else ci
            rows = pl.ds(pl.multiple_of(cidx * CHUNK, CHUNK), CHUNK)
            qs = _silu(q_ref[0, rows, :]) * HG_DK ** -0.5
            vc = v_ref[0, rows, :]
            f_raw = f_ref[0, rows, :]
            log_sig = jnp.minimum(f_raw, 0.0) - jnp.log1p(jnp.exp(-jnp.abs(f_raw)))
            t = log_1m + log_sig
            log_f = jnp.maximum(log_lb, t) + jnp.log1p(jnp.exp(-jnp.abs(log_lb - t)))
            kc = 1.0 - jnp.exp(log_f)
            cum = jnp.dot(tri, log_f, precision=HIGHEST, preferred_element_type=F32)
            total = cum[0:1, :] if rev else cum[CHUNK - 1:CHUNK, :]

            a_mat = jnp.where(eye, jnp.sum(qs * kc, axis=-1, keepdims=True), 0.0)
            b_m = cum
            for lvl in range(6):
                m = 1 << lvl
                prev_b = pltpu.roll(b_m, (CHUNK - m) if rev else m, axis=0)
                qt = qs * jnp.exp(jnp.minimum(cum - prev_b, 0.0))
                kt = kc * jnp.exp(jnp.minimum(b_m - cum, 0.0))
                a_lvl = _dot_nt(qt.astype(BF16), kt.astype(BF16))
                grp_r = tau_r >> lvl
                mask = jnp.logical_and((grp_r & 1) == 1, (tau_c >> lvl) == grp_r - 1)
                a_mat = a_mat + jnp.where(mask, a_lvl, 0.0)
                if lvl < 5:
                    nxt = pltpu.roll(b_m, m if rev else (CHUNK - m), axis=0)
                    b_m = jnp.where(((tau_row >> lvl) & 1) == 1, b_m, nxt)

            st = st_scr[...]
            o_c = (jnp.dot(a_mat.astype(BF16), vc.astype(BF16), preferred_element_type=F32)
                   + _dot_nt((qs * jnp.exp(cum)).astype(BF16), st.astype(BF16)))
            kd = kc * jnp.exp(total - cum)
            st_scr[...] = st * jnp.exp(total) + _dot_tn(vc.astype(BF16), kd.astype(BF16))

            if not rev:
                ofwd_scr[rows, :] = o_c
            else:
                tot = ofwd_scr[rows, :] + o_c
                y = tot * lax.rsqrt(jnp.mean(tot * tot, axis=-1, keepdims=True) + EPS) * nrm
                o_ref[0, rows, :] = (y * _silu(g_ref[0, rows, :])).astype(o_ref.dtype)
            return carry

        lax.fori_loop(0, n_chunks, chunk_body, 0)
        if want_state:
            sfin_ref[0, d, 0] = st_scr[...].T


def gla_pallas(proj, lb, hg_norm, s0, want_state):
    bsz, seq_len, _ = proj.shape
    n_chunks = seq_len // CHUNK
    has_s0 = s0 is not None

    def col(group):
        return pl.BlockSpec((1, seq_len, LANES), lambda b, h, group=group: (b, 0, group * HG_HEADS + h))

    in_specs = [col(0), col(1), col(2), col(3), col(4),
                pl.BlockSpec((2, LANES), lambda b, h: (0, h)),
                pl.BlockSpec((1, LANES), lambda b, h: (0, 0))]
    args = [proj, proj, proj, proj, proj, lb, hg_norm.reshape(1, LANES)]
    if has_s0:
        in_specs.append(pl.BlockSpec((1, 2, 1, HG_DK, HG_DV), lambda b, h: (b, 0, h, 0, 0)))
        args.append(s0)
    out_specs = [pl.BlockSpec((1, seq_len, LANES), lambda b, h: (b, 0, h))]
    out_shape = [jax.ShapeDtypeStruct((bsz, seq_len, MIX_W), BF16)]
    if want_state:
        out_specs.append(pl.BlockSpec((1, 2, 1, HG_DK, HG_DV), lambda b, h: (b, 0, h, 0, 0)))
        out_shape.append(jax.ShapeDtypeStruct((bsz, 2, HG_HEADS, HG_DK, HG_DV), F32))
    res = pl.pallas_call(
        functools.partial(_gla_kernel, n_chunks=n_chunks, has_s0=has_s0, want_state=want_state),
        grid=(bsz, HG_HEADS),
        in_specs=in_specs, out_specs=out_specs, out_shape=out_shape,
        scratch_shapes=[pltpu.VMEM((seq_len, LANES), F32), pltpu.VMEM((HG_DV, HG_DK), F32)],
        compiler_params=_cparams("arbitrary", "arbitrary"),
        name="hgrn2_scan",
    )(*args)
    return (res[0], res[1]) if want_state else (res[0], None)


def _swap_halves64(x):
    lane = lax.broadcasted_iota(jnp.int32, x.shape, 1)
    return jnp.where((lane & 63) < 32, pltpu.roll(x, 96, axis=1), pltpu.roll(x, 32, axis=1))


def _mla_kernel(*refs, seq_len, past, latent):
    cq_ref, ckv_ref, kpe_ref, g_ref, qn_ref, kvn_ref, wuq_ref, wukv_ref = refs[:8]
    pos = 8
    if latent:
        ctx_ckv_ref, ctx_kpe_ref, cosk_ref, sink_ref, cosq_ref, sinq_ref = refs[pos:pos + 6]
        pos += 6
    o_ref = refs[pos]
    pos += 1
    if not latent:
        ckv_out_ref, kpe_out_ref = refs[pos:pos + 2]
        pos += 2
    kn_scr, v_scr, kpe2_scr = refs[pos:pos + 3]

    @pl.when(pl.program_id(1) == 0)
    def _build_keys():
        wukv = wukv_ref[...]

        def put(rows_bf16, base, n):
            kv = jnp.dot(rows_bf16, wukv, preferred_element_type=F32)
            for h in range(MLA_HEADS):
                kn_scr[h, base:base + n, :] = kv[:, h * 256:h * 256 + 128].astype(BF16)
                v_scr[h, base:base + n, :] = kv[:, h * 256 + 128:(h + 1) * 256].astype(BF16)

        if latent:
            put(ctx_ckv_ref[0, 0].astype(BF16), 0, past)
            kpe2_scr[0:past, :] = ctx_kpe_ref[0, 0].astype(BF16)
        kvn = kvn_ref[...]
        for r0 in range(0, seq_len, KV_ROWS):
            n = min(KV_ROWS, seq_len - r0)
            c = ckv_ref[0, r0:r0 + n, :]
            cn = c * lax.rsqrt(jnp.mean(c * c, axis=-1, keepdims=True) + EPS) * kvn
            if not latent:
                ckv_out_ref[0, r0:r0 + n, :] = cn
            put(cn.astype(BF16), past + r0, n)
        kp = kpe_ref[0]
        if latent:
            kp = kp * cosk_ref[...] + _swap_halves64(kp) * sink_ref[...]
        else:
            kpe_out_ref[0] = kp[:, :MLA_ROPE]
        kpe2_scr[past:past + seq_len, :] = (kp + pltpu.roll(kp, MLA_ROPE, axis=1)).astype(BF16)

    cq = cq_ref[0]
    cqn = cq * lax.rsqrt(jnp.mean(cq * cq, axis=-1, keepdims=True) + EPS) * qn_ref[...]
    q = jnp.dot(cqn.astype(BF16), wuq_ref[...], preferred_element_type=F32)
    lane = lax.broadcasted_iota(jnp.int32, (q.shape[0], LANES), 1)
    kpe2 = kpe2_scr[...]
    for h in range(MLA_HEADS):
        qn = q[:, h * 128:(h + 1) * 128]
        p0 = MLA_HEADS * MLA_NOPE + (h // 2) * LANES
        qp = q[:, p0:p0 + LANES]
        if latent:
            qp = qp * cosq_ref[...] + _swap_halves64(qp) * sinq_ref[...]
        qp = jnp.where((lane >> 6) == (h % 2), qp, 0.0)
        s = (_dot_nt(qn.astype(BF16), kn_scr[h]) + _dot_nt(qp.astype(BF16), kpe2)) * MLA_SCALE
        e = jnp.exp(s - jnp.max(s, axis=-1, keepdims=True))
        p = e / jnp.sum(e, axis=-1, keepdims=True)
        o_h = jnp.dot(p.astype(BF16), v_scr[h], preferred_element_type=F32)
        o_ref[0, :, h * 128:(h + 1) * 128] = (o_h * _silu(g_ref[0, :, h * 128:(h + 1) * 128])).astype(o_ref.dtype)


def mla_pallas(proj, q_norm, kv_norm, w_uq_p, w_ukv, ctx_ckv=None, ctx_kpe2=None, e_idx=0, rope=None):
    bsz, seq_len, _ = proj.shape
    latent = ctx_ckv is not None
    past = ctx_ckv.shape[2] if latent else 0
    lk = past + seq_len
    tq = min(MLA_TQ, seq_len)
    nq = seq_len // tq
    in_specs = [
        pl.BlockSpec((1, tq, 512), lambda b, i: (b, i, 10)),
        pl.BlockSpec((1, seq_len, 512), lambda b, i: (b, 0, 11)),
        pl.BlockSpec((1, seq_len, LANES), lambda b, i: (b, 0, 56)),
        pl.BlockSpec((1, tq, 1024), lambda b, i: (b, i, 6)),
        pl.BlockSpec((1, 512), lambda b, i: (0, 0)),
        pl.BlockSpec((1, 512), lambda b, i: (0, 0)),
        pl.BlockSpec((512, 1536), lambda b, i: (0, 0)),
        pl.BlockSpec((512, 2048), lambda b, i: (0, 0)),
    ]
    args = [proj, proj, proj, proj, q_norm.reshape(1, 512), kv_norm.reshape(1, 512), w_uq_p, w_ukv]
    if latent:
        cos_t, sin_t = rope
        in_specs += [
            pl.BlockSpec((1, 1, past, 512), lambda b, i: (b, e_idx, 0, 0)),
            pl.BlockSpec((1, 1, past, LANES), lambda b, i: (b, e_idx, 0, 0)),
            pl.BlockSpec((seq_len, LANES), lambda b, i: (0, 0)),
            pl.BlockSpec((seq_len, LANES), lambda b, i: (0, 0)),
            pl.BlockSpec((tq, LANES), lambda b, i: (i, 0)),
            pl.BlockSpec((tq, LANES), lambda b, i: (i, 0)),
        ]
        args += [ctx_ckv, ctx_kpe2, cos_t, sin_t, cos_t, sin_t]
    out_specs = [pl.BlockSpec((1, tq, 1024), lambda b, i: (b, i, 0))]
    out_shape = [jax.ShapeDtypeStruct((bsz, seq_len, MIX_W), BF16)]
    if not latent:
        out_specs += [pl.BlockSpec((1, seq_len, 512), lambda b, i: (b, 0, 0)),
                      pl.BlockSpec((1, seq_len, MLA_ROPE), lambda b, i: (b, 0, 0))]
        out_shape += [jax.ShapeDtypeStruct((bsz, seq_len, MLA_KV_RANK), F32),
                      jax.ShapeDtypeStruct((bsz, seq_len, MLA_ROPE), F32)]
    return pl.pallas_call(
        functools.partial(_mla_kernel, seq_len=seq_len, past=past, latent=latent),
        grid=(bsz, nq),
        in_specs=in_specs, out_specs=out_specs, out_shape=out_shape,
        scratch_shapes=[pltpu.VMEM((MLA_HEADS, lk, LANES), BF16), pltpu.VMEM((MLA_HEADS, lk, LANES), BF16),
                        pltpu.VMEM((lk, LANES), BF16)],
        compiler_params=_cparams("arbitrary", "arbitrary"),
        name="mla_attention",
    )(*args)


def axial_rope(n_tokens):
    rows = n_tokens // GRID_W
    row = jnp.repeat(jnp.arange(rows, dtype=jnp.float32), GRID_W)
    col = jnp.arange(n_tokens, dtype=jnp.int32) % GRID_W
    half = MLA_ROPE // 2
    inv_freq = ROPE_BASE ** (-jnp.arange(0, half, 2, dtype=jnp.float32) / half)
    ang = jnp.concatenate([row[:, None] * inv_freq, col.astype(jnp.float32)[:, None] * inv_freq], axis=-1)
    return jnp.cos(ang), jnp.sin(ang)


def rope_tables(cos, sin):
    return jnp.tile(cos, (1, 4)), jnp.tile(jnp.concatenate([-sin, sin], axis=-1), (1, 2))


def permute_even_in(w):
    return jnp.concatenate([w[:, :6144], w[:, 6208:7232], w[:, 6144:6208],
                            jnp.zeros((w.shape[0], EVEN_IN_PAD - EVEN_IN), w.dtype)], axis=1)


def permute_w_uq(w):
    w3 = w.reshape(w.shape[0], MLA_HEADS, MLA_NOPE + MLA_ROPE)
    return jnp.concatenate([w3[:, :, :MLA_NOPE].reshape(w.shape[0], -1),
                            w3[:, :, MLA_NOPE:].reshape(w.shape[0], -1)], axis=1)


def _mask_matmul_exact(mask_bf16, x):
    x_hi = x.astype(BF16)
    r1 = x - x_hi.astype(F32)
    x_mid = r1.astype(BF16)
    x_lo = (r1 - x_mid.astype(F32)).astype(BF16)
    return (jnp.dot(mask_bf16, x_hi, preferred_element_type=F32) + jnp.dot(mask_bf16, x_mid, preferred_element_type=F32)
            + jnp.dot(mask_bf16, x_lo, preferred_element_type=F32))


GDN_GROUP = 4


def _gdn_kernel(*refs, seq_len, has_s0, want_state):
    q_ref, k_ref, v_ref, z_ref, ab_ref, wq_ref, wk_ref, wv_ref, alog_ref, dt_ref, nrm_ref = refs[:11]
    pos = 11
    s0_ref = None
    if has_s0:
        s0_ref = refs[pos]
        pos += 1
    o_ref = refs[pos]
    pos += 1
    sfin_ref = None
    if want_state:
        sfin_ref = refs[pos]
        pos += 1
    qn_scr, kn_scr, vn_scr, gb_scr = refs[pos:pos + 4]
    pos += 4
    per_dir = [refs[pos + 8 * d:pos + 8 * d + 8] for d in (0, 1)]
    n_chunks = seq_len // CHUNK
    head = pl.program_id(1)

    row = lax.broadcasted_iota(jnp.int32, (seq_len, LANES), 0)

    def conv_silu(x_ref, w_ref):
        x = x_ref[0]
        w = w_ref[...]
        x_prev = jnp.where(row == 0, 0.0, pltpu.roll(x, 1, axis=0))
        x_next = jnp.where(row == seq_len - 1, 0.0, pltpu.roll(x, seq_len - 1, axis=0))
        return _silu(x_prev * w[0:1, :] + x * w[1:2, :] + x_next * w[2:3, :])

    def l2n(x):
        return x * lax.rsqrt(jnp.sum(x * x, axis=-1, keepdims=True) + EPS)

    qn_scr[...] = l2n(conv_silu(q_ref, wq_ref)) * GDN_DK ** -0.5
    kn_scr[...] = l2n(conv_silu(k_ref, wk_ref))
    vn_scr[...] = conv_silu(v_ref, wv_ref)
    ab = ab_ref[0]
    lane_full = lax.broadcasted_iota(jnp.int32, (seq_len, LANES), 1)
    pre = ab + dt_ref[...]
    softplus = jnp.maximum(pre, 0.0) + jnp.log1p(jnp.exp(-jnp.abs(pre)))
    gb_scr[...] = jnp.where(lane_full < 2 * GDN_HEADS, -jnp.exp(alog_ref[...]) * softplus, jax.nn.sigmoid(ab))

    rows_g = GDN_GROUP * CHUNK
    ri = lax.broadcasted_iota(jnp.int32, (rows_g, rows_g), 0)
    ci_ = lax.broadcasted_iota(jnp.int32, (rows_g, rows_g), 1)
    same_blk = (ri // CHUNK) == (ci_ // CHUNK)
    lane = lax.broadcasted_iota(jnp.int32, (rows_g, LANES), 1)
    eye = ri == ci_
    eye_f = eye.astype(F32)

    def prepare(vals, d):
        rev = d == 1
        qc, kc, vc, gb = vals
        tau_r = (CHUNK - 1 - ri % CHUNK) if rev else ri % CHUNK
        tau_c = (CHUNK - 1 - ci_ % CHUNK) if rev else ci_ % CHUNK
        incl = jnp.logical_and(same_blk, tau_c <= tau_r)
        strict = jnp.logical_and(same_blk, tau_c < tau_r)
        later = jnp.logical_and(same_blk, tau_c > tau_r)
        g_lane = d * GDN_HEADS + head
        b_lane = 2 * GDN_HEADS + d * GDN_HEADS + head
        cum_all = _mask_matmul_exact(incl.astype(BF16), gb)
        rest_all = _mask_matmul_exact(later.astype(BF16), gb)
        cum_col = jnp.sum(jnp.where(lane == g_lane, cum_all, 0.0), axis=-1, keepdims=True)
        rest_col = jnp.sum(jnp.where(lane == g_lane, rest_all, 0.0), axis=-1, keepdims=True)
        beta_col = jnp.sum(jnp.where(lane == b_lane, gb, 0.0), axis=-1, keepdims=True)
        cum_row = jnp.sum(jnp.where(eye, cum_col, 0.0), axis=0, keepdims=True)
        decay = jnp.exp(jnp.minimum(cum_col - cum_row, 0.0))
        k_beta = kc * beta_col
        kc_bf = kc.astype(BF16)
        kk = _dot_nt(k_beta.astype(BF16), kc_bf)
        lower = jnp.where(strict, kk * decay, 0.0)
        t_mat = eye_f - jnp.where(jnp.logical_and((tau_r & 1) == 1, tau_c == tau_r - 1), lower, 0.0)
        for lvl in range(1, 6):
            grp_r = tau_r >> lvl
            mask = jnp.logical_and((grp_r & 1) == 1, (tau_c >> lvl) == grp_r - 1)
            c_m = jnp.where(mask, lower, 0.0)
            t_mat = t_mat - _dot3(_dot3(t_mat, c_m), t_mat)
        t_bf = t_mat.astype(BF16)
        e_cum = jnp.exp(cum_col)
        w = jnp.dot(t_bf, (k_beta * e_cum).astype(BF16), preferred_element_type=F32).astype(BF16)
        u = jnp.dot(t_bf, (vc * beta_col).astype(BF16), preferred_element_type=F32)
        attn = jnp.where(incl, _dot_nt(qc.astype(BF16), kc_bf) * decay, 0.0)
        qe = (qc * e_cum).astype(BF16)
        kd = (kc * jnp.exp(rest_col)).astype(BF16)
        e_tot = jnp.exp(cum_col + rest_col)
        return w, u, qe, kd, attn, e_tot

    def pre_body(gi, carry):
        rows = pl.ds(pl.multiple_of(gi * rows_g, rows_g), rows_g)
        vals = (qn_scr[rows, :], kn_scr[rows, :], vn_scr[rows, :], gb_scr[rows, :])
        res = [prepare(vals, d) for d in (0, 1)]
        for d in (0, 1):
            w, u, qe, kd, attn, e_tot = res[d]
            w_scr, u_scr, qe_scr, kd_scr, at_scr, et_scr = per_dir[d][:6]
            w_scr[rows, :], u_scr[rows, :], qe_scr[rows, :], kd_scr[rows, :] = w, u, qe, kd
            for j in range(GDN_GROUP):
                sub = pl.ds(pl.multiple_of(gi * rows_g + j * CHUNK, CHUNK), CHUNK)
                at_scr[sub, :] = attn[j * CHUNK:(j + 1) * CHUNK, j * CHUNK:(j + 1) * CHUNK].astype(BF16)
                et_scr[gi * GDN_GROUP + j] = jnp.broadcast_to(e_tot[j * CHUNK:j * CHUNK + 1, :], (1, LANES))
        return carry

    lax.fori_loop(0, seq_len // rows_g, pre_body, 0)

    for d in (0, 1):
        s_scr = per_dir[d][7]
        if has_s0:
            s_scr[...] = s0_ref[0, d, 0]
        else:
            s_scr[...] = jnp.zeros_like(s_scr)

    def load_step(cidx, d):
        w_scr, u_scr, qe_scr, kd_scr, at_scr, et_scr, _, s_scr = per_dir[d]
        rows = pl.ds(pl.multiple_of(cidx * CHUNK, CHUNK), CHUNK)
        return (w_scr[rows, :], u_scr[rows, :], qe_scr[rows, :], kd_scr[rows, :], at_scr[rows, :],
                et_scr[cidx], s_scr[...])

    def step(vals):
        w, u, qe, kd, attn, et, s = vals
        s_bf = s.astype(BF16)
        v_new = u - jnp.dot(w, s_bf, preferred_element_type=F32)
        v_new_bf = v_new.astype(BF16)
        o = jnp.dot(qe, s_bf, preferred_element_type=F32) + jnp.dot(attn, v_new_bf, preferred_element_type=F32)
        return o, et[:, 0:1] * s + _dot_tn(kd, v_new_bf)

    def rec_body(ci, carry):
        idx = (ci, n_chunks - 1 - ci)
        vals = [load_step(idx[d], d) for d in (0, 1)]
        res = [step(v) for v in vals]
        for d in (0, 1):
            rows = pl.ds(pl.multiple_of(idx[d] * CHUNK, CHUNK), CHUNK)
            per_dir[d][6][rows, :] = res[d][0]
            per_dir[d][7][...] = res[d][1]
        return carry

    lax.fori_loop(0, n_chunks, rec_body, 0)

    tot = per_dir[0][6][...] + per_dir[1][6][...]
    y = tot * lax.rsqrt(jnp.mean(tot * tot, axis=-1, keepdims=True) + EPS) * nrm_ref[...]
    o_ref[0] = (y * _silu(z_ref[0])).astype(o_ref.dtype)
    if want_state:
        for d in (0, 1):
            sfin_ref[0, d, 0] = per_dir[d][7][...]


def gdn_pallas(proj, conv_w, a_log, dt_bias, gdn_norm, s0, want_state):
    bsz, seq_len, _ = proj.shape
    has_s0 = s0 is not None

    def col(group):
        return pl.BlockSpec((1, seq_len, LANES), lambda b, h, group=group: (b, 0, group * GDN_HEADS + h))

    def wcol(group):
        return pl.BlockSpec((CONV_W, LANES), lambda b, h, group=group: (0, group * GDN_HEADS + h))

    vec = pl.BlockSpec((1, LANES), lambda b, h: (0, 0))
    pad = jnp.zeros((LANES - 2 * GDN_HEADS,), F32)
    alog_vec = jnp.concatenate([a_log.reshape(-1), pad]).reshape(1, LANES)
    dt_vec = jnp.concatenate([dt_bias.reshape(-1), pad]).reshape(1, LANES)
    gate_block = (3 * MIX_W + MIX_W + 3 * HY_CH + HY_CH) // LANES
    in_specs = [col(0), col(1), col(2), col(3),
                pl.BlockSpec((1, seq_len, LANES), lambda b, h: (b, 0, gate_block)),
                wcol(0), wcol(1), wcol(2), vec, vec, vec]
    args = [proj, proj, proj, proj, proj, conv_w, conv_w, conv_w, alog_vec, dt_vec, gdn_norm.reshape(1, LANES)]
    if has_s0:
        in_specs.append(pl.BlockSpec((1, 2, 1, GDN_DK, GDN_DV), lambda b, h: (b, 0, h, 0, 0)))
        args.append(s0)
    out_specs = [pl.BlockSpec((1, seq_len, LANES), lambda b, h: (b, 0, h))]
    out_shape = [jax.ShapeDtypeStruct((bsz, seq_len, MIX_W), BF16)]
    if want_state:
        out_specs.append(pl.BlockSpec((1, 2, 1, GDN_DK, GDN_DV), lambda b, h: (b, 0, h, 0, 0)))
        out_shape.append(jax.ShapeDtypeStruct((bsz, 2, GDN_HEADS, GDN_DK, GDN_DV), F32))
    seq_f32 = pltpu.VMEM((seq_len, LANES), F32)
    seq_bf16 = pltpu.VMEM((seq_len, LANES), BF16)
    one_dir = [seq_bf16, seq_f32, seq_bf16, seq_bf16, pltpu.VMEM((seq_len, CHUNK), BF16),
               pltpu.VMEM((seq_len // CHUNK, 1, LANES), F32), seq_f32, pltpu.VMEM((GDN_DK, GDN_DV), F32)]
    res = pl.pallas_call(
        functools.partial(_gdn_kernel, seq_len=seq_len, has_s0=has_s0, want_state=want_state),
        grid=(bsz, GDN_HEADS),
        in_specs=in_specs, out_specs=out_specs, out_shape=out_shape,
        scratch_shapes=[seq_f32, seq_f32, seq_f32, seq_f32] + one_dir + one_dir,
        compiler_params=_cparams("arbitrary", "arbitrary"),
        name="gdn_scan",
    )(*args)
    return (res[0], res[1]) if want_state else (res[0], None)


def permute_odd_in(w):
    return jnp.concatenate([w[:, :3072], w[:, 3104:4128], w[:, 4128:7200], w[:, 7200:8224], w[:, 3072:3104],
                            jnp.zeros((w.shape[0], ODD_IN_PAD - ODD_IN), w.dtype)], axis=1)


def _mm3(m_hi, m_lo, x_hi, x_lo):
    return (jnp.dot(m_hi, x_hi, preferred_element_type=F32) + jnp.dot(m_hi, x_lo, preferred_element_type=F32)
            + jnp.dot(m_lo, x_hi, preferred_element_type=F32))


def _mm3_tn(m_hi, m_lo, x_hi, x_lo):
    return _dot_tn(m_hi, x_hi) + _dot_tn(m_hi, x_lo) + _dot_tn(m_lo, x_hi)


def dft_mats(seq_len):
    n = 2 * seq_len
    k = jnp.arange(seq_len, dtype=jnp.int32)[:, None]
    t = jnp.arange(seq_len, dtype=jnp.int32)[None, :]
    ang = ((k * t) & (n - 1)).astype(F32) * (2.0 * math.pi / n)
    return _split_bf16(jnp.cos(ang)) + _split_bf16(jnp.sin(ang))


def filter_features(n):
    t = jnp.linspace(0.0, 1.0, n, dtype=F32)[:, None]
    w = 2.0 * math.pi * jnp.arange(n, dtype=F32)[:, None] / n
    f = jnp.linspace(1e-4, HY_BANDS - 1, HY_BANDS, dtype=F32)[None, :]
    z = jnp.concatenate([t, jnp.cos(f * w), -jnp.sin(f * w)], axis=-1)
    return jnp.pad(z, ((0, 0), (0, LANES - HY_EMB))), t


def filter_deltas():
    max_decay = math.log(HY_TARGET) / HY_DECAY_PCT_SHORT
    min_decay = math.log(HY_TARGET) / HY_DECAY_PCT_LONG
    return jnp.abs(jnp.linspace(min_decay, max_decay, HY_CH, dtype=F32))[None, :]


def _filter_kernel(z_ref, t_ref, w1_ref, b1_ref, w2_ref, b2_ref, w3f_ref, w3b_ref, dl_ref,
                   chi_ref, clo_ref, shi_ref, slo_ref, hr_ref, hi_ref, hn_ref,
                   sum_hi, sum_lo, dif_hi, dif_lo):
    @pl.when(pl.program_id(1) == 0)
    def _make_filter():
        hid = jnp.sin(jnp.dot(z_ref[...].astype(BF16), w1_ref[...].astype(BF16), preferred_element_type=F32)
                      + b1_ref[...])
        hid = jnp.sin(jnp.dot(hid.astype(BF16), w2_ref[...].astype(BF16), preferred_element_type=F32)
                      + b2_ref[...])
        hid = hid.astype(BF16)
        win = jnp.exp(-t_ref[...] * dl_ref[...]) + HY_SHIFT
        hf = jnp.dot(hid, w3f_ref[...].astype(BF16), preferred_element_type=F32) * win
        hb = jnp.dot(hid, w3b_ref[...].astype(BF16), preferred_element_type=F32) * win
        row = lax.broadcasted_iota(jnp.int32, hb.shape, 0)
        hb = jnp.where(row == 0, 0.0, hb)
        inv = 1.0 / jnp.sum(jnp.abs(hf) + jnp.abs(hb), axis=0, keepdims=True)
        hsum = (hf + hb) * inv
        hdif = (hb - hf) * inv
        sign = (1 - 2 * (row & 1)).astype(F32)
        hn_ref[...] = jnp.sum(hsum * sign, axis=0, keepdims=True)
        sum_hi[...], sum_lo[...] = _split_bf16(hsum)
        dif_hi[...], dif_lo[...] = _split_bf16(hdif)

    hr_ref[...] = _mm3(chi_ref[...], clo_ref[...], sum_hi[...], sum_lo[...])
    hi_ref[...] = _mm3(shi_ref[...], slo_ref[...], dif_hi[...], dif_lo[...])


def hyena_filter_spectrum(seq_len, w1, b1, w2, b2, w3, mats):
    z, t = filter_features(seq_len)
    w1p = jnp.pad(w1, ((0, LANES - HY_EMB), (0, 0)))
    kb = min(HY_KB, seq_len)
    nj = HY_CH // HY_TC

    def full(shape):
        return pl.BlockSpec(shape, lambda j, k: (0,) * len(shape))

    mat = pl.BlockSpec((kb, seq_len), lambda j, k: (k, 0))
    return pl.pallas_call(
        _filter_kernel,
        grid=(nj, seq_len // kb),
        in_specs=[full((seq_len, LANES)), full((seq_len, 1)), full((LANES, HY_HIDDEN)), full((1, HY_HIDDEN)),
                  full((HY_HIDDEN, HY_HIDDEN)), full((1, HY_HIDDEN)),
                  pl.BlockSpec((HY_HIDDEN, HY_TC), lambda j, k: (0, j)),
                  pl.BlockSpec((HY_HIDDEN, HY_TC), lambda j, k: (0, nj + j)),
                  pl.BlockSpec((1, HY_TC), lambda j, k: (0, j)),
                  mat, mat, mat, mat],
        out_specs=[pl.BlockSpec((kb, HY_TC), lambda j, k: (k, j)),
                   pl.BlockSpec((kb, HY_TC), lambda j, k: (k, j)),
                   pl.BlockSpec((1, HY_TC), lambda j, k: (0, j))],
        out_shape=[jax.ShapeDtypeStruct((seq_len, HY_CH), F32), jax.ShapeDtypeStruct((seq_len, HY_CH), F32),
                   jax.ShapeDtypeStruct((1, HY_CH), F32)],
        scratch_shapes=[pltpu.VMEM((seq_len, HY_TC), BF16)] * 4,
        compiler_params=_cparams("arbitrary", "arbitrary"),
        name="hyena_filter",
    )(z, t, w1p, b1.reshape(1, -1), w2, b2.reshape(1, -1), w3, w3, filter_deltas(), *mats)


def _hyena_kernel(v_ref, x0_ref, x1_ref, g_ref, wv_ref, w0_ref, w1_ref, bv_ref, b0_ref, b1_ref, skip_ref,
                  hr_ref, hi_ref, hn_ref, chi_ref, clo_ref, shi_ref, slo_ref, o_ref,
                  u_scr, uhi_scr, ulo_scr, x0g_scr, y_scr, unyq_scr, *, seq_len, kb):
    kk = pl.program_id(2)

    @pl.when(kk == 0)
    def _prologue():
        row = lax.broadcasted_iota(jnp.int32, (seq_len, HY_TC), 0)

        def conv(x_ref, w_ref, b_ref):
            x = x_ref[0]
            w = w_ref[...]
            x_prev = jnp.where(row == 0, 0.0, pltpu.roll(x, 1, axis=0))
            x_next = jnp.where(row == seq_len - 1, 0.0, pltpu.roll(x, seq_len - 1, axis=0))
            return x_prev * w[0:1, :] + x * w[1:2, :] + x_next * w[2:3, :] + b_ref[...]

        u = conv(v_ref, wv_ref, bv_ref) * conv(x1_ref, w1_ref, b1_ref)
        u_scr[...] = u
        uhi_scr[...], ulo_scr[...] = _split_bf16(u)
        x0g_scr[...] = conv(x0_ref, w0_ref, b0_ref) * _silu(g_ref[0])
        sign = (1 - 2 * (row & 1)).astype(F32)
        unyq_scr[...] = jnp.sum(u * sign, axis=0, keepdims=True)
        y_scr[...] = jnp.zeros_like(y_scr)

    chi, clo, shi, slo = chi_ref[...], clo_ref[...], shi_ref[...], slo_ref[...]
    a = _mm3(chi, clo, uhi_scr[...], ulo_scr[...])
    b = _mm3(shi, slo, uhi_scr[...], ulo_scr[...])
    hr, hi = hr_ref[...], hi_ref[...]
    grow = kk * kb + lax.broadcasted_iota(jnp.int32, a.shape, 0)
    pr = (a * hr + b * hi) * jnp.where(grow == 0, 1.0, 2.0)
    pi = (b * hr - a * hi) * 2.0
    pr_hi, pr_lo = _split_bf16(pr)
    pi_hi, pi_lo = _split_bf16(pi)
    y_scr[...] += _mm3_tn(chi, clo, pr_hi, pr_lo) + _mm3_tn(shi, slo, pi_hi, pi_lo)

    @pl.when(kk == pl.num_programs(2) - 1)
    def _epilogue():
        row = lax.broadcasted_iota(jnp.int32, (seq_len, HY_TC), 0)
        sign = (1 - 2 * (row & 1)).astype(F32)
        y = (y_scr[...] + sign * (unyq_scr[...] * hn_ref[...])) * (1.0 / (2 * seq_len))
        o_ref[0] = (x0g_scr[...] * (y + u_scr[...] * skip_ref[...])).astype(o_ref.dtype)


def hyena_pallas(proj, conv_w, conv_b, skip, spectrum, mats):
    bsz, seq_len, _ = proj.shape
    hr, hi, hn = spectrum
    kb = min(HY_KB, seq_len)
    nj = HY_CH // HY_TC
    base = 4 * MIX_W // HY_TC

    def col(group):
        return pl.BlockSpec((1, seq_len, HY_TC), lambda j, b, k, group=group: (b, 0, base + group * nj + j))

    def wcol(group):
        return pl.BlockSpec((CONV_W, HY_TC), lambda j, b, k, group=group: (0, group * nj + j))

    def bcol(group):
        return pl.BlockSpec((1, HY_TC), lambda j, b, k, group=group: (0, group * nj + j))

    mat = pl.BlockSpec((kb, seq_len), lambda j, b, k: (k, 0))
    spec = pl.BlockSpec((kb, HY_TC), lambda j, b, k: (k, j))
    vec = pl.BlockSpec((1, HY_TC), lambda j, b, k: (0, j))
    conv_b2 = conv_b.reshape(1, -1)
    return pl.pallas_call(
        functools.partial(_hyena_kernel, seq_len=seq_len, kb=kb),
        grid=(nj, bsz, seq_len // kb),
        in_specs=[col(0), col(1), col(2), col(3), wcol(0), wcol(1), wcol(2), bcol(0), bcol(1), bcol(2), vec,
                  spec, spec, vec, mat, mat, mat, mat],
        out_specs=pl.BlockSpec((1, seq_len, HY_TC), lambda j, b, k: (b, 0, j)),
        out_shape=jax.ShapeDtypeStruct((bsz, seq_len, HY_CH), BF16),
        scratch_shapes=[pltpu.VMEM((seq_len, HY_TC), F32), pltpu.VMEM((seq_len, HY_TC), BF16),
                        pltpu.VMEM((seq_len, HY_TC), BF16), pltpu.VMEM((seq_len, HY_TC), F32),
                        pltpu.VMEM((seq_len, HY_TC), F32), pltpu.VMEM((1, HY_TC), F32)],
        compiler_params=_cparams("arbitrary", "arbitrary", "arbitrary"),
        name="hyena_conv",
    )(proj, proj, proj, proj, conv_w, conv_w, conv_w, conv_b2, conv_b2, conv_b2, skip.reshape(1, -1),
      hr, hi, hn, *mats)


def kernel(x_prompt, x_sample, state_hgrn, cache_mla_ckv, cache_mla_kpe, state_gdn, c, c_ctx,
           ada_w, ada_b, norm_pre, norm_post, ev_in_w, ev_out_w, hg_lb_logits, hg_norm,
           mla_q_norm, mla_w_uq, mla_kv_norm, mla_w_ukv, od_in_w, od_out_w, gdn_conv, gdn_a_log,
           gdn_dt_bias, gdn_norm, hy_conv_w, hy_conv_b, hy_w1, hy_b1, hy_w2, hy_b2, hy_w3, hy_skip):
    bp, lp, d = x_prompt.shape
    bs, ls, _ = x_sample.shape

    lb_cum = jnp.cumsum(jax.nn.softmax(hg_lb_logits.astype(jnp.float32), axis=1), axis=1)
    hg_lb = lb_cum - lb_cum[:, :1]

    cond = jnp.concatenate([c_ctx[None], c, jnp.zeros((8 - 1 - bs, d), c.dtype)], axis=0)
    mods = adaln_all(cond, ada_w, ada_b)

    rope = rope_tables(*axial_rope(ls))
    ctx_kpe2 = jnp.concatenate([cache_mla_kpe, cache_mla_kpe], axis=-1)
    mats_p, mats_s = dft_mats(lp), dft_mats(ls)

    xp = x_prompt.reshape(bp * lp, d)
    xs = x_sample.reshape(bs * ls, d)
    new_hg, new_ckv, new_kpe, new_gdn = [], [], [], []
    for l in range(DEPTH):
        shift, scale, gate = jnp.split(mods[l], 3, axis=-1)
        shift_p, scale_p, gate_p = (t[0:1, None, :] for t in (shift, scale, gate))
        shift_s, scale_s, gate_s = (t[1:1 + bs, None, :] for t in (shift, scale, gate))
        gain_pre = norm_pre[l][None]
        gain_post = norm_post[l][None]
        if l % 2 == 0:
            e = l // 2
            w_in = permute_even_in(ev_in_w[e]).astype(BF16)
            w_out = ev_out_w[e].astype(BF16)
            tn = EVEN_TN
        else:
            o = l // 2
            w_in = permute_odd_in(od_in_w[o]).astype(BF16)
            w_out = od_out_w[o].astype(BF16)
            tn = ODD_TN
        proj_p = in_proj(xp, gain_pre, scale_p, shift_p, w_in, tn, bp * lp).reshape(bp, lp, -1)
        proj_s = in_proj(xs, gain_pre, scale_s, shift_s, w_in, tn, ls).reshape(bs, ls, -1)
        if l % 2 == 0:
            lb = hg_lb[:, l]
            w_uq = permute_w_uq(mla_w_uq[e]).astype(BF16)
            w_ukv = mla_w_ukv[e].astype(BF16)
            oa_p, s_hg = gla_pallas(proj_p, lb, hg_norm[e], None, True)
            oa_s, _ = gla_pallas(proj_s, lb, hg_norm[e], state_hgrn[:, e], False)
            ob_p, ckv, kpe = mla_pallas(proj_p, mla_q_norm[e], mla_kv_norm[e], w_uq, w_ukv)
            ob_s, = mla_pallas(proj_s, mla_q_norm[e], mla_kv_norm[e], w_uq, w_ukv,
                               cache_mla_ckv, ctx_kpe2, e, rope)
            new_hg.append(s_hg)
            new_ckv.append(ckv)
            new_kpe.append(kpe)
        else:
            oa_p, s_gdn = gdn_pallas(proj_p, gdn_conv[o], gdn_a_log[o], gdn_dt_bias[o], gdn_norm[o], None, True)
            oa_s, _ = gdn_pallas(proj_s, gdn_conv[o], gdn_a_log[o], gdn_dt_bias[o], gdn_norm[o],
                                 state_gdn[:, o], False)
            fw = (hy_w1[o], hy_b1[o], hy_w2[o], hy_b2[o], hy_w3[o])
            ob_p = hyena_pallas(proj_p, hy_conv_w[o], hy_conv_b[o], hy_skip[o],
                                hyena_filter_spectrum(lp, *fw, mats_p), mats_p)
            ob_s = hyena_pallas(proj_s, hy_conv_w[o], hy_conv_b[o], hy_skip[o],
                                hyena_filter_spectrum(ls, *fw, mats_s), mats_s)
            new_gdn.append(s_gdn)
        xp = out_proj(oa_p.reshape(bp * lp, -1), ob_p.reshape(bp * lp, -1), w_out, gain_post, gate_p, xp, bp * lp)
        xs = out_proj(oa_s.reshape(bs * ls, -1), ob_s.reshape(bs * ls, -1), w_out, gain_post, gate_s, xs, ls)

    return (xp.reshape(bp, lp, d), xs.reshape(bs, ls, d), jnp.stack(new_hg, axis=1),
            jnp.stack(new_ckv, axis=1), jnp.stack(new_kpe, axis=1), jnp.stack(new_gdn, axis=1))
```
